```python
import math
import jax, jax.numpy as jnp
from jax import lax
import numpy as np

D_MODEL = 1024
BATCH = 2
SEQ = 16384
DEPTH = 2

GRID_W = 64
CTX_LEN = 256
Q_BLOCK = 128
ROPE_THETA = 10000.0
EPS = 1e-6
N_BRANCH = 3

A_HEAD_DIM = 64
A_WIDTH = D_MODEL // 2
A_HEADS = A_WIDTH // A_HEAD_DIM
A_KV_HEADS = A_HEADS // 4
A_GROUP = A_HEADS // A_KV_HEADS
A_KV_WIDTH = A_KV_HEADS * A_HEAD_DIM

B_WIDTH = D_MODEL // 4
B_GROUP_DIM = 64
B_GROUPS = B_WIDTH // B_GROUP_DIM
B_CHUNK = 128

C_WIDTH = D_MODEL // 4
C_QK_DIM = 32
C_V_DIM = 2 * C_QK_DIM
C_HEADS = C_WIDTH // C_V_DIM
C_QK_WIDTH = C_HEADS * 2 * C_QK_DIM

SPLIT_SIZES = (A_WIDTH, C_QK_WIDTH, A_KV_WIDTH, A_KV_WIDTH, C_QK_WIDTH, C_WIDTH,
               A_WIDTH, C_WIDTH, B_WIDTH, B_WIDTH, B_WIDTH, N_BRANCH * D_MODEL)
SPLIT_OFFSETS = tuple(int(o) for o in np.cumsum(SPLIT_SIZES)[:-1])
PROJ_WIDTH = int(sum(SPLIT_SIZES))
KV_START = A_WIDTH + C_QK_WIDTH
KV_END = KV_START + 2 * A_KV_WIDTH + C_QK_WIDTH + C_WIDTH
KV_OFFSETS = (A_KV_WIDTH, 2 * A_KV_WIDTH, 2 * A_KV_WIDTH + C_QK_WIDTH)

kernel_name = 'hybrid_gqa_sgu_diffattn_prefix_dit'


def rms_norm(x, g):
    xf = x.astype(jnp.float32)
    y = xf * lax.rsqrt(jnp.mean(xf * xf, axis=-1, keepdims=True) + EPS)
    return (y * g.astype(jnp.float32)).astype(x.dtype)


def modulate(x, g, shift, scale):
    return rms_norm(x, g) * (1.0 + scale) + shift


def axial_rope_tables(rows, cols, dim):
    quarter = dim // 4
    inv = jnp.power(ROPE_THETA, -jnp.arange(quarter, dtype=jnp.float32) / quarter)
    ang = jnp.concatenate([rows[:, None].astype(jnp.float32) * inv,
                           cols[:, None].astype(jnp.float32) * inv], axis=-1)
    return jnp.cos(ang)[None, :, None, :], jnp.sin(ang)[None, :, None, :]


def apply_rope(x, cos, sin):
    xf = x.astype(jnp.float32).reshape(*x.shape[:-1], x.shape[-1] // 2, 2)
    x0, x1 = xf[..., 0], xf[..., 1]
    out = jnp.stack([x0 * cos - x1 * sin, x0 * sin + x1 * cos], axis=-1)
    return out.reshape(x.shape).astype(x.dtype)


def prep_kv(a_k, a_v, c_k, c_v, gk_a, gk_c, rope_a, rope_c):
    b, t = a_k.shape[:2]
    ka = rms_norm(a_k.reshape(b, t, A_KV_HEADS, A_HEAD_DIM), gk_a)
    kc = rms_norm(c_k.reshape(b, t, 2 * C_HEADS, C_QK_DIM), gk_c)
    if rope_a is not None:
        ka = apply_rope(ka, *rope_a)
        kc = apply_rope(kc, *rope_c)
    return (ka, a_v.reshape(b, t, A_KV_HEADS, A_HEAD_DIM),
            kc.reshape(b, t, C_HEADS, 2, C_QK_DIM), c_v.reshape(b, t, C_HEADS, C_V_DIM))


def prep_q(a_q, c_q, gq_a, gq_c, rope_a, rope_c):
    b, t = a_q.shape[:2]
    qa = rms_norm(a_q.reshape(b, t, A_HEADS, A_HEAD_DIM), gq_a)
    qc = rms_norm(c_q.reshape(b, t, 2 * C_HEADS, C_QK_DIM), gq_c)
    if rope_a is not None:
        qa = apply_rope(qa, *rope_a)
        qc = apply_rope(qc, *rope_c)
    return (qa.reshape(b, t, A_KV_HEADS, A_GROUP, A_HEAD_DIM),
            qc.reshape(b, t, C_HEADS, 2, C_QK_DIM))


def gqa_attend(q, k, v):
    s = jnp.einsum('bqhgd,bkhd->bhgqk', q, k).astype(jnp.float32) * (A_HEAD_DIM ** -0.5)
    p = jax.nn.softmax(s, axis=-1).astype(v.dtype)
    return jnp.einsum('bhgqk,bkhd->bqhgd', p, v)


def diff_attend(q, k, v, lam):
    s = jnp.einsum('bqhmd,bkhmd->bhmqk', q, k).astype(jnp.float32) * (C_QK_DIM ** -0.5)
    p = jax.nn.softmax(s, axis=-1)
    p = p[:, :, 0] - lam * p[:, :, 1]
    return jnp.einsum('bhqk,bkhe->bqhe', p.astype(v.dtype), v)


def diff_post(o, g_subln, lam_init):
    b, t = o.shape[:2]
    return (rms_norm(o, g_subln) * (1.0 - lam_init)).reshape(b, t, C_WIDTH)


def sweep_query_blocks(fn, q):
    b, t = q.shape[:2]
    nb = t // Q_BLOCK
    qb = jnp.moveaxis(q.reshape(b, nb, Q_BLOCK, *q.shape[2:]), 1, 0)
    ob = lax.map(fn, qb)
    return jnp.moveaxis(ob, 0, 1).reshape(b, t, *ob.shape[3:])


def spatial_gate(u, v, g_sgu, w_sp, b_sp):
    b, t, _ = v.shape
    vn = rms_norm(v.reshape(b, t, B_GROUPS, B_GROUP_DIM), g_sgu)
    vc = vn.reshape(b, t // B_CHUNK, B_CHUNK, B_GROUPS, B_GROUP_DIM)
    mixed = jnp.einsum('gpq,bnqgc->bnpgc', w_sp, vc) + b_sp.T[:, :, None]
    return u * mixed.reshape(b, t, B_WIDTH)


def merge_branches(o_a, o_b, o_c, z_a, z_b, z_c, gates, w_pa, w_pb, w_pc, w_out):
    y_a = (o_a * jax.nn.silu(z_a)) @ w_pa
    y_b = (o_b * jax.nn.silu(z_b)) @ w_pb
    y_c = (o_c * jax.nn.silu(z_c)) @ w_pc
    g = jax.nn.sigmoid(gates.reshape(*gates.shape[:-1], N_BRANCH, D_MODEL))
    m = g[..., 0, :] * y_a + g[..., 1, :] * y_b + g[..., 2, :] * y_c
    return m @ w_out


def trunk_layer(x, ctx, c, c_ctx, w_mod, b_mod, g_norm, w_in, gq_a, gk_a, gq_c, gk_c,
                g_sgu, w_sp, b_sp, lam_c, g_subln, w_pa, w_pb, w_pc, w_out,
                lam_init, update_ctx, rope_a, rope_c):
    b, t, _ = x.shape
    mod = jax.nn.silu(c) @ w_mod + b_mod
    shift, scale, gate = jnp.split(mod[:, None, :], 3, axis=-1)
    mod_ctx = jax.nn.silu(c_ctx) @ w_mod + b_mod
    shift_c, scale_c, gate_c = jnp.split(mod_ctx, 3)

    hn = modulate(x, g_norm, shift, scale)
    (a_q, c_q, a_k, a_v, c_k, c_v, z_a, z_c, b_u, b_v, b_z, gates) = jnp.split(
        hn @ w_in, SPLIT_OFFSETS, axis=-1)

    hn_ctx = modulate(ctx, g_norm, shift_c, scale_c)
    if update_ctx:
        parts_ctx = jnp.split(hn_ctx @ w_in, SPLIT_OFFSETS, axis=-1)
        kv_ctx_parts = parts_ctx[2:6]
    else:
        kv_ctx_parts = jnp.split(hn_ctx @ w_in[:, KV_START:KV_END], KV_OFFSETS, axis=-1)
    ka_x, va_x, kc_x, vc_x = prep_kv(*kv_ctx_parts, gk_a, gk_c, None, None)
    ka_l, va_l, kc_l, vc_l = prep_kv(a_k, a_v, c_k, c_v, gk_a, gk_c, rope_a, rope_c)
    ka = jnp.concatenate([ka_x, ka_l], axis=1)
    va = jnp.concatenate([va_x, va_l], axis=1)
    kc = jnp.concatenate([kc_x, kc_l], axis=1)
    vc = jnp.concatenate([vc_x, vc_l], axis=1)
    qa, qc = prep_q(a_q, c_q, gq_a, gq_c, rope_a, rope_c)

    lam_f = lam_c.astype(jnp.float32)
    lam = (jnp.exp(jnp.sum(lam_f[0] * lam_f[1])) - jnp.exp(jnp.sum(lam_f[2] * lam_f[3]))
           + lam_init)

    o_a = sweep_query_blocks(lambda qb: gqa_attend(qb, ka, va), qa).reshape(b, t, A_WIDTH)
    o_c = diff_post(sweep_query_blocks(lambda qb: diff_attend(qb, kc, vc, lam), qc),
                    g_subln, lam_init)
    o_b = spatial_gate(b_u, b_v, g_sgu, w_sp, b_sp)
    x_new = x + gate * merge_branches(o_a, o_b, o_c, z_a, b_z, z_c, gates,
                                      w_pa, w_pb, w_pc, w_out)

    if update_ctx:
        qa_x, qc_x = prep_q(parts_ctx[0], parts_ctx[1], gq_a, gq_c, None, None)
        oa_x = gqa_attend(qa_x, ka_x, va_x).reshape(b, ctx.shape[1], A_WIDTH)
        oc_x = diff_post(diff_attend(qc_x, kc_x, vc_x, lam), g_subln, lam_init)
        ob_x = spatial_gate(parts_ctx[8], parts_ctx[9], g_sgu, w_sp, b_sp)
        ctx = ctx + gate_c * merge_branches(oa_x, ob_x, oc_x, parts_ctx[6], parts_ctx[10],
                                            parts_ctx[7], parts_ctx[11],
                                            w_pa, w_pb, w_pc, w_out)
    return x_new, ctx


def setup_inputs(seed: int = 0) -> dict:
    key = jax.random.key(seed)
    ks = jax.random.split(key, 24)
    f32 = jnp.float32
    nrm = lambda k, shape, s: jax.random.normal(k, shape, f32) * s
    gain = lambda k, shape: 1.0 + 0.02 * jax.random.normal(k, shape, f32)
    return {
        'x': nrm(ks[0], (BATCH, SEQ, D_MODEL), 1.0),
        'c': nrm(ks[1], (BATCH, D_MODEL), 1.0),
        'ctx': nrm(ks[2], (BATCH, CTX_LEN, D_MODEL), 1.0),
        'c_ctx': nrm(ks[3], (D_MODEL,), 1.0),
        'w_mod': nrm(ks[4], (DEPTH, D_MODEL, 3 * D_MODEL), 0.5 * D_MODEL ** -0.5),
        'b_mod': nrm(ks[5], (DEPTH, 3 * D_MODEL), 0.02),
        'g_norm': gain(ks[6], (DEPTH, D_MODEL)),
        'w_in': nrm(ks[7], (DEPTH, D_MODEL, PROJ_WIDTH), D_MODEL ** -0.5),
        'gq_a': gain(ks[8], (DEPTH, A_HEAD_DIM)),
        'gk_a': gain(ks[9], (DEPTH, A_HEAD_DIM)),
        'gq_c': gain(ks[10], (DEPTH, C_QK_DIM)),
        'gk_c': gain(ks[11], (DEPTH, C_QK_DIM)),
        'g_sgu': gain(ks[12], (DEPTH, B_GROUPS, B_GROUP_DIM)),
        'w_sp': nrm(ks[13], (DEPTH, B_GROUPS, B_CHUNK, B_CHUNK), B_CHUNK ** -0.5),
        'b_sp': nrm(ks[14], (DEPTH, B_GROUPS, B_CHUNK), 0.02),
        'lam_c': nrm(ks[15], (DEPTH, 4, C_QK_DIM), 0.1),
        'g_subln': gain(ks[16], (DEPTH, C_V_DIM)),
        'w_pa': nrm(ks[17], (DEPTH, A_WIDTH, D_MODEL), A_WIDTH ** -0.5),
        'w_pb': nrm(ks[18], (DEPTH, B_WIDTH, D_MODEL), B_WIDTH ** -0.5),
        'w_pc': nrm(ks[19], (DEPTH, C_WIDTH, D_MODEL), C_WIDTH ** -0.5),
        'w_out': nrm(ks[20], (DEPTH, D_MODEL, D_MODEL), D_MODEL ** -0.5),
    }


def reference(x, c, ctx, c_ctx, w_mod, b_mod, g_norm, w_in, gq_a, gk_a, gq_c, gk_c,
              g_sgu, w_sp, b_sp, lam_c, g_subln, w_pa, w_pb, w_pc, w_out):
    n_tok = x.shape[1]
    ROWS = n_tok // GRID_W
    rows = jnp.repeat(jnp.arange(ROWS), GRID_W)
    cols = jnp.tile(jnp.arange(GRID_W), ROWS)
    rope_a = axial_rope_tables(rows, cols, A_HEAD_DIM)
    rope_c = axial_rope_tables(rows, cols, C_QK_DIM)
    for l in range(DEPTH):
        lam_init = 0.8 - 0.6 * math.exp(-0.3 * l)
        x, ctx = trunk_layer(x, ctx, c, c_ctx, w_mod[l], b_mod[l], g_norm[l], w_in[l],
                             gq_a[l], gk_a[l], gq_c[l], gk_c[l], g_sgu[l], w_sp[l], b_sp[l],
                             lam_c[l], g_subln[l], w_pa[l], w_pb[l], w_pc[l], w_out[l],
                             lam_init, l < DEPTH - 1, rope_a, rope_c)
    return x
```

```python
import functools
import math

import numpy as np
import jax
import jax.numpy as jnp
from jax import lax
from jax.experimental import pallas as pl
from jax.experimental.pallas import tpu as pltpu

F32 = jnp.float32
BF16 = jnp.bfloat16

D_MODEL = 1024
GRID_W = 64
ROPE_THETA = 10000.0
EPS = 1e-6
N_BRANCH = 3

A_HEAD_DIM = 64
A_WIDTH = D_MODEL // 2
A_HEADS = A_WIDTH // A_HEAD_DIM
A_KV_HEADS = A_HEADS // 4
A_GROUP = A_HEADS // A_KV_HEADS
A_KV_WIDTH = A_KV_HEADS * A_HEAD_DIM

B_WIDTH = D_MODEL // 4
B_GROUP_DIM = 64
B_GROUPS = B_WIDTH // B_GROUP_DIM
B_CHUNK = 128

C_WIDTH = D_MODEL // 4
C_QK_DIM = 32
C_V_DIM = 2 * C_QK_DIM
C_HEADS = C_WIDTH // C_V_DIM
C_QK_WIDTH = C_HEADS * 2 * C_QK_DIM

PROJ_WIDTH = 6144
R_QA, R_QC, R_KA, R_KC, R_VA, R_VC, R_REST = 0, 512, 768, 896, 1152, 1280, 1536
REST_WIDTH = PROJ_WIDTH - R_REST
S_ZA, S_ZC, S_BU, S_BV, S_BZ, S_GATES = 0, 512, 768, 1024, 1280, 1536

V_ROWS = 80
SGU_BD = 2 * B_CHUNK
VMEM_LIMIT = 56 * 1024 * 1024


def _pick(n, candidates):
    for c in candidates:
        if n % c == 0:
            return c
    raise ValueError(f"no tile for {n} in {candidates}")


def _silu(x):
    return x / (1.0 + jnp.exp(-x))


def _sigmoid(x):
    return 1.0 / (1.0 + jnp.exp(-x))


def _mod_kernel(c_ref, w_ref, b_ref, o_ref):
    s = _silu(c_ref[...])
    o_ref[0] = jnp.dot(s, w_ref[0], preferred_element_type=F32,
                       precision=lax.Precision.HIGHEST) + b_ref[0]


def _modulation(cs, w_mod, b_mod):
    depth = w_mod.shape[0]
    nj = 3 * D_MODEL // 1024
    return pl.pallas_call(
        _mod_kernel,
        grid=(depth, nj),
        in_specs=[pl.BlockSpec((8, D_MODEL), lambda l, j: (0, 0)),
                  pl.BlockSpec((1, D_MODEL, 1024), lambda l, j: (l, 0, j)),
                  pl.BlockSpec((1, 1, 1024), lambda l, j: (l, 0, j))],
        out_specs=pl.BlockSpec((1, 8, 1024), lambda l, j: (l, 0, j)),
        out_shape=jax.ShapeDtypeStruct((depth, 8, 3 * D_MODEL), F32),
        compiler_params=pltpu.CompilerParams(
            dimension_semantics=("parallel", "parallel"), vmem_limit_bytes=VMEM_LIMIT),
        name="adaln_mod",
    )(cs, w_mod, b_mod.reshape(depth, 1, 3 * D_MODEL))


def _norm_rope(p, g, cos, sin):
    half = p.shape[0] // 2
    ms = jnp.mean(p * p, axis=0, keepdims=True)
    y = p * lax.rsqrt(ms + EPS) * g
    x0, x1 = y[:half], y[half:]
    return jnp.concatenate([x0 * cos - x1 * sin, x0 * sin + x1 * cos], axis=0)


def _proj_kernel(x_ref, shift_ref, scale_ref, gn_ref, w_ref, gqa_ref, gka_ref, gqc_ref, gkc_ref,
                 cosa_ref, sina_ref, cosc_ref, sinc_ref,
                 qa_ref, qc_ref, ka_ref, kc_ref, va_ref, vc_ref, rest_ref):
    x = x_ref[0]
    ms = jnp.mean(x * x, axis=0, keepdims=True)
    y = x * lax.rsqrt(ms + EPS) * gn_ref[...]
    hn = (y * (1.0 + scale_ref[0]) + shift_ref[0]).astype(BF16)

    def proj(r0, r1):
        return jnp.dot(w_ref[r0:r1, :], hn, preferred_element_type=F32)

    cosa, sina, cosc, sinc = cosa_ref[...], sina_ref[...], cosc_ref[...], sinc_ref[...]

    pqa = proj(R_QA, R_QC)
    for h in range(A_HEADS):
        r = h * A_HEAD_DIM
        qa_ref[0, r:r + A_HEAD_DIM, :] = _norm_rope(
            pqa[r:r + A_HEAD_DIM], gqa_ref[...], cosa, sina).astype(BF16)
    pqc = proj(R_QC, R_KA)
    for u in range(2 * C_HEADS):
        r = u * C_QK_DIM
        qc_ref[0, r:r + C_QK_DIM, :] = _norm_rope(
            pqc[r:r + C_QK_DIM], gqc_ref[...], cosc, sinc).astype(BF16)
    pka = proj(R_KA, R_KC)
    ka = jnp.concatenate(
        [_norm_rope(pka[h * A_HEAD_DIM:(h + 1) * A_HEAD_DIM], gka_ref[...], cosa, sina)
         for h in range(A_KV_HEADS)], axis=0)
    ka_ref[0] = ka.T.astype(BF16)
    pkc = proj(R_KC, R_VA)
    kc = jnp.concatenate(
        [_norm_rope(pkc[u * C_QK_DIM:(u + 1) * C_QK_DIM], gkc_ref[...], cosc, sinc)
         for u in range(2 * C_HEADS)], axis=0)
    kc_ref[0] = kc.T.astype(BF16)
    va_ref[0] = proj(R_VA, R_VC).astype(BF16)
    vc_ref[0] = proj(R_VC, R_REST).astype(BF16)
    for r in range(R_REST, PROJ_WIDTH, 512):
        rest_ref[0, r - R_REST:r - R_REST + 512, :] = proj(r, r + 512).astype(BF16)


def _project(xT, shift, scale, g_norm, w_t, gqa, gka, gqc, gkc, cosa, sina, cosc, sinc):
    b, _, t = xT.shape
    tm = _pick(t, (512, 256, 128))
    col = lambda n: pl.BlockSpec((n, 1), lambda bi, i: (0, 0))
    tok = lambda n: pl.BlockSpec((n, tm), lambda bi, i: (0, i))
    outT = lambda n: pl.BlockSpec((1, n, tm), lambda bi, i: (bi, 0, i))
    return pl.pallas_call(
        _proj_kernel,
        grid=(b, t // tm),
        in_specs=[pl.BlockSpec((1, D_MODEL, tm), lambda bi, i: (bi, 0, i)),
                  pl.BlockSpec((1, D_MODEL, 1), lambda bi, i: (bi, 0, 0)),
                  pl.BlockSpec((1, D_MODEL, 1), lambda bi, i: (bi, 0, 0)),
                  col(D_MODEL),
                  pl.BlockSpec((PROJ_WIDTH, D_MODEL), lambda bi, i: (0, 0),
                               pipeline_mode=pl.Buffered(1)),
                  col(A_HEAD_DIM), col(A_HEAD_DIM), col(C_QK_DIM), col(C_QK_DIM),
                  tok(A_HEAD_DIM // 2), tok(A_HEAD_DIM // 2), tok(C_QK_DIM // 2), tok(C_QK_DIM // 2)],
        out_specs=[outT(A_WIDTH), outT(C_QK_WIDTH),
                   pl.BlockSpec((1, tm, A_KV_WIDTH), lambda bi, i: (bi, i, 0)),
                   pl.BlockSpec((1, tm, C_QK_WIDTH), lambda bi, i: (bi, i, 0)),
                   outT(A_KV_WIDTH), outT(C_WIDTH), outT(REST_WIDTH)],
        out_shape=[jax.ShapeDtypeStruct((b, A_WIDTH, t), BF16),
                   jax.ShapeDtypeStruct((b, C_QK_WIDTH, t), BF16),
                   jax.ShapeDtypeStruct((b, t, A_KV_WIDTH), BF16),
                   jax.ShapeDtypeStruct((b, t, C_QK_WIDTH), BF16),
                   jax.ShapeDtypeStruct((b, A_KV_WIDTH, t), BF16),
                   jax.ShapeDtypeStruct((b, C_WIDTH, t), BF16),
                   jax.ShapeDtypeStruct((b, REST_WIDTH, t), BF16)],
        compiler_params=pltpu.CompilerParams(
            dimension_semantics=("parallel", "parallel"), vmem_limit_bytes=VMEM_LIMIT),
        name="in_proj",
    )(xT, shift, scale, g_norm, w_t, gqa, gka, gqc, gkc, cosa, sina, cosc, sinc)


def _attn_kernel(lam_ref, gsub_ref, qa_ref, qc_ref, ka_ref, kc_ref, va_ref, vc_ref,
                 oa_ref, oc_ref, acc_a, m_a, acc_c, m_c, *, lam_init):
    kv = pl.program_id(2)

    @pl.when(kv == 0)
    def _init():
        acc_a[...] = jnp.zeros_like(acc_a)
        acc_c[...] = jnp.zeros_like(acc_c)
        m_a[...] = jnp.full_like(m_a, -1e30)
        m_c[...] = jnp.full_like(m_c, -1e30)

    def update(acc_ref, m_ref, u, k, qpad, v_aug):
        s = jnp.dot(k, qpad, preferred_element_type=F32)
        m_prev = m_ref[u:u + 1, :]
        m_new = jnp.maximum(m_prev, jnp.max(s, axis=0, keepdims=True))
        p = jnp.exp(s - m_new).astype(BF16)
        alpha = jnp.exp(m_prev - m_new)
        acc_ref[u] = acc_ref[u] * alpha + jnp.dot(v_aug, p, preferred_element_type=F32)
        m_ref[u:u + 1, :] = m_new

    def padded(q, u, n_units):
        z = jnp.zeros_like(q)
        return jnp.concatenate([q if j == u else z for j in range(n_units)], axis=0)

    ka = ka_ref[0]
    for h in range(A_KV_HEADS):
        for g in range(A_GROUP):
            hd = h * A_GROUP + g
            q = qa_ref[0, hd * A_HEAD_DIM:(hd + 1) * A_HEAD_DIM, :]
            update(acc_a, m_a, hd, ka, padded(q, h, A_KV_HEADS), va_ref[0, h])
    kc = kc_ref[0]
    for u in range(2 * C_HEADS):
        q = qc_ref[0, u * C_QK_DIM:(u + 1) * C_QK_DIM, :]
        update(acc_c, m_c, u, kc, padded(q, u, 2 * C_HEADS), vc_ref[0, u // 2])

    @pl.when(kv == pl.num_programs(2) - 1)
    def _finish():
        for hd in range(A_HEADS):
            acc = acc_a[hd]
            oa_ref[0, hd * A_HEAD_DIM:(hd + 1) * A_HEAD_DIM, :] = (
                acc[:A_HEAD_DIM] / acc[A_HEAD_DIM:A_HEAD_DIM + 1]).astype(BF16)
        lam_c = lam_ref[...]
        lam = (jnp.exp(jnp.sum(lam_c[0:1] * lam_c[1:2], axis=1, keepdims=True))
               - jnp.exp(jnp.sum(lam_c[2:3] * lam_c[3:4], axis=1, keepdims=True)) + lam_init)
        for h in range(C_HEADS):
            a1, a2 = acc_c[2 * h], acc_c[2 * h + 1]
            o = (a1[:C_V_DIM] / a1[C_V_DIM:C_V_DIM + 1]
                 - lam * (a2[:C_V_DIM] / a2[C_V_DIM:C_V_DIM + 1]))
            ms = jnp.mean(o * o, axis=0, keepdims=True)
            o = o * lax.rsqrt(ms + EPS) * gsub_ref[...] * (1.0 - lam_init)
            oc_ref[0, h * C_V_DIM:(h + 1) * C_V_DIM, :] = o.astype(BF16)


def _attention(lam_c, g_subln, qa, qc, ka, kc, va, vc, lam_init):
    b, _, t = qa.shape
    tkv = ka.shape[1]
    tq = _pick(t, (256, 128))
    tk = _pick(tkv, (1280, 1024, 512, 256))
    return pl.pallas_call(
        functools.partial(_attn_kernel, lam_init=lam_init),
        grid=(b, t // tq, tkv // tk),
        in_specs=[pl.BlockSpec((4, C_QK_DIM), lambda bi, i, j: (0, 0)),
                  pl.BlockSpec((C_V_DIM, 1), lambda bi, i, j: (0, 0)),
                  pl.BlockSpec((1, A_WIDTH, tq), lambda bi, i, j: (bi, 0, i)),
                  pl.BlockSpec((1, C_QK_WIDTH, tq), lambda bi, i, j: (bi, 0, i)),
                  pl.BlockSpec((1, tk, A_KV_WIDTH), lambda bi, i, j: (bi, j, 0)),
                  pl.BlockSpec((1, tk, C_QK_WIDTH), lambda bi, i, j: (bi, j, 0)),
                  pl.BlockSpec((1, A_KV_HEADS, V_ROWS, tk), lambda bi, i, j: (bi, 0, 0, j)),
                  pl.BlockSpec((1, C_HEADS, V_ROWS, tk), lambda bi, i, j: (bi, 0, 0, j))],
        out_specs=[pl.BlockSpec((1, A_WIDTH, tq), lambda bi, i, j: (bi, 0, i)),
                   pl.BlockSpec((1, C_WIDTH, tq), lambda bi, i, j: (bi, 0, i))],
        out_shape=[jax.ShapeDtypeStruct((b, A_WIDTH, t), BF16),
                   jax.ShapeDtypeStruct((b, C_WIDTH, t), BF16)],
        scratch_shapes=[pltpu.VMEM((A_HEADS, V_ROWS, tq), F32),
                        pltpu.VMEM((A_HEADS, tq), F32),
                        pltpu.VMEM((2 * C_HEADS, V_ROWS, tq), F32),
                        pltpu.VMEM((2 * C_HEADS, tq), F32)],
        compiler_params=pltpu.CompilerParams(
            dimension_semantics=("parallel", "parallel", "arbitrary"),
            vmem_limit_bytes=VMEM_LIMIT),
        name="attention",
    )(lam_c, g_subln, qa, qc, ka, kc, va, vc)


def _merge_kernel(x_ref, gate_ref, oa_ref, oc_ref, rest_ref, gsgu_ref, wsp_ref, bsp_ref,
                  wpa_ref, wpb_ref, wpc_ref, wout_ref, o_ref):
    tm = x_ref.shape[2]
    rest = lambda r0, n: rest_ref[0, r0:r0 + n, :].astype(F32)

    ha = (oa_ref[0].astype(F32) * _silu(rest(S_ZA, A_WIDTH))).astype(BF16)
    ya = jnp.dot(wpa_ref[...], ha, preferred_element_type=F32)
    hc = (oc_ref[0].astype(F32) * _silu(rest(S_ZC, C_WIDTH))).astype(BF16)
    yc = jnp.dot(wpc_ref[...], hc, preferred_element_type=F32)

    mixed_groups = []
    for g in range(B_GROUPS):
        v = rest(S_BV + g * B_GROUP_DIM, B_GROUP_DIM)
        ms = jnp.mean(v * v, axis=0, keepdims=True)
        vn = (v * lax.rsqrt(ms + EPS) * gsgu_ref[g]).astype(BF16)
        cols = [jnp.dot(vn[:, c:c + SGU_BD], wsp_ref[g], preferred_element_type=F32) + bsp_ref[g]
                for c in range(0, tm, SGU_BD)]
        mixed_groups.append(cols[0] if len(cols) == 1 else jnp.concatenate(cols, axis=1))
    mixed = jnp.concatenate(mixed_groups, axis=0)
    hb = (rest(S_BU, B_WIDTH) * mixed * _silu(rest(S_BZ, B_WIDTH))).astype(BF16)
    yb = jnp.dot(wpb_ref[...], hb, preferred_element_type=F32)

    m = (_sigmoid(rest(S_GATES, D_MODEL)) * ya
         + _sigmoid(rest(S_GATES + D_MODEL, D_MODEL)) * yb
         + _sigmoid(rest(S_GATES + 2 * D_MODEL, D_MODEL)) * yc)
    out = jnp.dot(wout_ref[...], m.astype(BF16), preferred_element_type=F32)
    o_ref[0] = x_ref[0] + gate_ref[0] * out


def _merge(xT, gate, oa, oc, rest, gsgu, wsp_bd, bsp, wpa_t, wpb_t, wpc_t, wout_t):
    b, _, t = xT.shape
    tm = _pick(t, (512, 256))
    tokT = lambda n: pl.BlockSpec((1, n, tm), lambda bi, i: (bi, 0, i))
    full = lambda shape: pl.BlockSpec(shape, lambda bi, i: (0,) * len(shape))
    return pl.pallas_call(
        _merge_kernel,
        grid=(b, t // tm),
        in_specs=[tokT(D_MODEL),
                  pl.BlockSpec((1, D_MODEL, 1), lambda bi, i: (bi, 0, 0)),
                  tokT(A_WIDTH), tokT(C_WIDTH), tokT(REST_WIDTH),
                  full((B_GROUPS, B_GROUP_DIM, 1)), full((B_GROUPS, SGU_BD, SGU_BD)),
                  full((B_GROUPS, 1, SGU_BD)),
                  full((D_MODEL, A_WIDTH)), full((D_MODEL, B_WIDTH)), full((D_MODEL, C_WIDTH)),
                  full((D_MODEL, D_MODEL))],
        out_specs=tokT(D_MODEL),
        out_shape=jax.ShapeDtypeStruct((b, D_MODEL, t), F32),
        compiler_params=pltpu.CompilerParams(
            dimension_semantics=("parallel", "parallel"), vmem_limit_bytes=VMEM_LIMIT),
        name="merge_out",
    )(xT, gate, oa, oc, rest, gsgu, wsp_bd, bsp, wpa_t, wpb_t, wpc_t, wout_t)


def _deinterleave(n_heads, dim):
    idx = np.arange(n_heads * dim).reshape(n_heads, dim // 2, 2)
    return np.concatenate([idx[:, :, 0], idx[:, :, 1]], axis=1).reshape(-1)


def _row_order():
    off = np.cumsum([0, A_WIDTH, C_QK_WIDTH, A_KV_WIDTH, A_KV_WIDTH, C_QK_WIDTH, C_WIDTH])
    qa = off[0] + _deinterleave(A_HEADS, A_HEAD_DIM)
    qc = off[1] + _deinterleave(2 * C_HEADS, C_QK_DIM)
    ka = off[2] + _deinterleave(A_KV_HEADS, A_HEAD_DIM)
    va = off[3] + np.arange(A_KV_WIDTH)
    kc = off[4] + _deinterleave(2 * C_HEADS, C_QK_DIM)
    vc = off[5] + np.arange(C_WIDTH)
    rest = np.arange(off[6], PROJ_WIDTH)
    return np.concatenate([qa, qc, ka, kc, va, vc, rest])


def _rope_tables(t, dim):
    quarter = dim // 4
    tok = jnp.arange(t)
    inv = jnp.power(ROPE_THETA, -jnp.arange(quarter, dtype=F32) / quarter)
    ang = jnp.concatenate([(tok // GRID_W)[:, None].astype(F32) * inv,
                           (tok % GRID_W)[:, None].astype(F32) * inv], axis=-1)
    return jnp.cos(ang).T, jnp.sin(ang).T


def _with_ones(vT, n_heads):
    b, _, t = vT.shape
    v = vT.reshape(b, n_heads, -1, t)
    ones = jnp.ones((b, n_heads, V_ROWS - v.shape[2], t), vT.dtype)
    return jnp.concatenate([v, ones], axis=2)


def kernel(x, c, ctx, c_ctx, w_mod, b_mod, g_norm, w_in, gq_a, gk_a, gq_c, gk_c, g_sgu, w_sp,
           b_sp, lam_c, g_subln, w_pa, w_pb, w_pc, w_out):
    depth = w_in.shape[0]
    b, t, _ = x.shape
    n_ctx = ctx.shape[1]

    cs = jnp.zeros((8, D_MODEL), F32).at[:b].set(c).at[b].set(c_ctx)
    mod = _modulation(cs, w_mod, b_mod)

    rows = _row_order()
    da, dc = _deinterleave(1, A_HEAD_DIM), _deinterleave(1, C_QK_DIM)
    cosa, sina = _rope_tables(t, A_HEAD_DIM)
    cosc, sinc = _rope_tables(t, C_QK_DIM)
    one = lambda n: jnp.ones((n, n_ctx), F32)
    zero = lambda n: jnp.zeros((n, n_ctx), F32)
    eye2 = jnp.eye(SGU_BD // B_CHUNK, dtype=F32)

    xT = jnp.transpose(x, (0, 2, 1))
    cT = jnp.transpose(ctx, (0, 2, 1))
    for l in range(depth):
        lam_init = 0.8 - 0.6 * math.exp(-0.3 * l)
        colv = lambda v: v.reshape(-1, D_MODEL, 1)
        shift, scale, gate = (colv(mod[l, :b, i * D_MODEL:(i + 1) * D_MODEL]) for i in range(3))
        shift_c, scale_c, gate_c = (
            colv(jnp.broadcast_to(mod[l, b, i * D_MODEL:(i + 1) * D_MODEL], (b, D_MODEL)))
            for i in range(3))
        w_t = w_in[l].T[rows].astype(BF16)
        gn = g_norm[l].reshape(D_MODEL, 1)
        gqa = (gq_a[l][da] * A_HEAD_DIM ** -0.5).reshape(-1, 1)
        gka = gk_a[l][da].reshape(-1, 1)
        gqc = (gq_c[l][dc] * C_QK_DIM ** -0.5).reshape(-1, 1)
        gkc = gk_c[l][dc].reshape(-1, 1)
        proj = functools.partial(_project, g_norm=gn, w_t=w_t, gqa=gqa, gka=gka, gqc=gqc, gkc=gkc)

        qa, qc, ka, kc, va, vc, rest = proj(
            xT, shift, scale, cosa=cosa, sina=sina, cosc=cosc, sinc=sinc)
        qa_x, qc_x, ka_x, kc_x, va_x, vc_x, rest_x = proj(
            cT, shift_c, scale_c, cosa=one(A_HEAD_DIM // 2), sina=zero(A_HEAD_DIM // 2),
            cosc=one(C_QK_DIM // 2), sinc=zero(C_QK_DIM // 2))

        gsub = g_subln[l].reshape(C_V_DIM, 1)
        va_x1, vc_x1 = _with_ones(va_x, A_KV_HEADS), _with_ones(vc_x, C_HEADS)
        oa, oc = _attention(
            lam_c[l], gsub, qa, qc,
            jnp.concatenate([ka_x, ka], axis=1), jnp.concatenate([kc_x, kc], axis=1),
            jnp.concatenate([va_x1, _with_ones(va, A_KV_HEADS)], axis=3),
            jnp.concatenate([vc_x1, _with_ones(vc, C_HEADS)], axis=3), lam_init)

        wsp_bd = jnp.einsum("ij,gqp->giqjp", eye2, jnp.swapaxes(w_sp[l], 1, 2)).reshape(
            B_GROUPS, SGU_BD, SGU_BD).astype(BF16)
        bsp = jnp.tile(b_sp[l], (1, SGU_BD // B_CHUNK)).reshape(B_GROUPS, 1, SGU_BD)
        merge = functools.partial(
            _merge, gsgu=g_sgu[l].reshape(B_GROUPS, B_GROUP_DIM, 1), wsp_bd=wsp_bd, bsp=bsp,
            wpa_t=w_pa[l].T.astype(BF16), wpb_t=w_pb[l].T.astype(BF16),
            wpc_t=w_pc[l].T.astype(BF16), wout_t=w_out[l].T.astype(BF16))
        xT_new = merge(xT, gate, oa, oc, rest)
        if l < depth - 1:
            oa_x, oc_x = _attention(lam_c[l], gsub, qa_x, qc_x, ka_x, kc_x, va_x1, vc_x1, lam_init)
            cT = merge(cT, gate_c, oa_x, oc_x, rest_x)
        xT = xT_new
    return jnp.transpose(xT, (0, 2, 1))
```

```python
import functools
import math

import numpy as np
import jax
import jax.numpy as jnp
from jax import lax
from jax.experimental import pallas as pl
from jax.experimental.pallas import tpu as pltpu

F32 = jnp.float32
BF16 = jnp.bfloat16

D_MODEL = 1024
GRID_W = 64
ROPE_THETA = 10000.0
EPS = 1e-6
N_BRANCH = 3

A_HEAD_DIM = 64
A_WIDTH = D_MODEL // 2
A_HEADS = A_WIDTH // A_HEAD_DIM
A_KV_HEADS = A_HEADS // 4
A_GROUP = A_HEADS // A_KV_HEADS
A_KV_WIDTH = A_KV_HEADS * A_HEAD_DIM

B_WIDTH = D_MODEL // 4
B_GROUP_DIM = 64
B_GROUPS = B_WIDTH // B_GROUP_DIM
B_CHUNK = 128

C_WIDTH = D_MODEL // 4
C_QK_DIM = 32
C_V_DIM = 2 * C_QK_DIM
C_HEADS = C_WIDTH // C_V_DIM
C_QK_WIDTH = C_HEADS * 2 * C_QK_DIM

PROJ_WIDTH = 6144
R_QA, R_QC, R_KA, R_KC, R_VA, R_VC, R_REST = 0, 512, 768, 896, 1152, 1280, 1536
REST_WIDTH = PROJ_WIDTH - R_REST
S_ZA, S_ZC, S_BU, S_BV, S_BZ, S_GATES = 0, 512, 768, 1024, 1280, 1536

V_ROWS = 80
SGU_BD = 2 * B_CHUNK
VMEM_LIMIT = 56 * 1024 * 1024
LOG2E = math.log2(math.e)

MIN_DENOM = 2.0 ** -80
STAB_MARGIN = 40.0


def _pick(n, candidates):
    for c in candidates:
        if n % c == 0:
            return c
    raise ValueError(f"no tile for {n} in {candidates}")


def _silu(x):
    return x / (1.0 + jnp.exp(-x))


def _sigmoid(x):
    return 1.0 / (1.0 + jnp.exp(-x))


def _mod_kernel(c_ref, w_ref, b_ref, o_ref):
    s = _silu(c_ref[...])
    o_ref[0] = jnp.dot(s, w_ref[0], preferred_element_type=F32,
                       precision=lax.Precision.HIGHEST) + b_ref[0]


def _modulation(cs, w_mod, b_mod):
    depth = w_mod.shape[0]
    nj = 3 * D_MODEL // 1024
    return pl.pallas_call(
        _mod_kernel,
        grid=(depth, nj),
        in_specs=[pl.BlockSpec((8, D_MODEL), lambda l, j: (0, 0)),
                  pl.BlockSpec((1, D_MODEL, 1024), lambda l, j: (l, 0, j)),
                  pl.BlockSpec((1, 1, 1024), lambda l, j: (l, 0, j))],
        out_specs=pl.BlockSpec((1, 8, 1024), lambda l, j: (l, 0, j)),
        out_shape=jax.ShapeDtypeStruct((depth, 8, 3 * D_MODEL), F32),
        compiler_params=pltpu.CompilerParams(
            dimension_semantics=("parallel", "parallel"), vmem_limit_bytes=VMEM_LIMIT),
        name="adaln_mod",
    )(cs, w_mod, b_mod.reshape(depth, 1, 3 * D_MODEL))


def _norm_rope(p, g, cos, sin):
    half = p.shape[0] // 2
    ms = jnp.mean(p * p, axis=0, keepdims=True)
    y = p * lax.rsqrt(ms + EPS) * g
    x0, x1 = y[:half], y[half:]
    return jnp.concatenate([x0 * cos - x1 * sin, x0 * sin + x1 * cos], axis=0)


def _proj_kernel(x_ref, shift_ref, scale_ref, gn_ref, w_ref, gqa_ref, gka_ref, gqc_ref, gkc_ref,
                 cosa_ref, sina_ref, cosc_ref, sinc_ref,
                 qa_ref, qc_ref, ka_ref, kc_ref, va_ref, vc_ref, rest_ref):
    x = x_ref[0]
    ms = jnp.mean(x * x, axis=0, keepdims=True)
    y = x * lax.rsqrt(ms + EPS) * gn_ref[...]
    hn = (y * (1.0 + scale_ref[0]) + shift_ref[0]).astype(BF16)

    def proj(r0, r1):
        return jnp.dot(w_ref[r0:r1, :], hn, preferred_element_type=F32)

    cosa, sina, cosc, sinc = cosa_ref[...], sina_ref[...], cosc_ref[...], sinc_ref[...]

    pqa = proj(R_QA, R_QC)
    for h in range(A_HEADS):
        r = h * A_HEAD_DIM
        qa_ref[0, r:r + A_HEAD_DIM, :] = _norm_rope(
            pqa[r:r + A_HEAD_DIM], gqa_ref[...], cosa, sina).astype(BF16)
    pqc = proj(R_QC, R_KA)
    for u in range(2 * C_HEADS):
        r = u * C_QK_DIM
        qc_ref[0, r:r + C_QK_DIM, :] = _norm_rope(
            pqc[r:r + C_QK_DIM], gqc_ref[...], cosc, sinc).astype(BF16)
    pka = proj(R_KA, R_KC)
    ka = jnp.concatenate(
        [_norm_rope(pka[h * A_HEAD_DIM:(h + 1) * A_HEAD_DIM], gka_ref[...], cosa, sina)
         for h in range(A_KV_HEADS)], axis=0)
    ka_ref[0] = ka.T.astype(BF16)
    pkc = proj(R_KC, R_VA)
    kc = jnp.concatenate(
        [_norm_rope(pkc[u * C_QK_DIM:(u + 1) * C_QK_DIM], gkc_ref[...], cosc, sinc)
         for u in range(2 * C_HEADS)], axis=0)
    kc_ref[0] = kc.T.astype(BF16)
    va_ref[0] = proj(R_VA, R_VC).astype(BF16)
    vc_ref[0] = proj(R_VC, R_REST).astype(BF16)
    for r in range(R_REST, PROJ_WIDTH, 512):
        rest_ref[0, r - R_REST:r - R_REST + 512, :] = proj(r, r + 512).astype(BF16)


def _project(xT, shift, scale, g_norm, w_t, gqa, gka, gqc, gkc, cosa, sina, cosc, sinc):
    b, _, t = xT.shape
    tm = _pick(t, (512, 256, 128))
    col = lambda n: pl.BlockSpec((n, 1), lambda bi, i: (0, 0))
    tok = lambda n: pl.BlockSpec((n, tm), lambda bi, i: (0, i))
    outT = lambda n: pl.BlockSpec((1, n, tm), lambda bi, i: (bi, 0, i))
    return pl.pallas_call(
        _proj_kernel,
        grid=(b, t // tm),
        in_specs=[pl.BlockSpec((1, D_MODEL, tm), lambda bi, i: (bi, 0, i)),
                  pl.BlockSpec((1, D_MODEL, 1), lambda bi, i: (bi, 0, 0)),
                  pl.BlockSpec((1, D_MODEL, 1), lambda bi, i: (bi, 0, 0)),
                  col(D_MODEL),
                  pl.BlockSpec((PROJ_WIDTH, D_MODEL), lambda bi, i: (0, 0),
                               pipeline_mode=pl.Buffered(1)),
                  col(A_HEAD_DIM), col(A_HEAD_DIM), col(C_QK_DIM), col(C_QK_DIM),
                  tok(A_HEAD_DIM // 2), tok(A_HEAD_DIM // 2), tok(C_QK_DIM // 2), tok(C_QK_DIM // 2)],
        out_specs=[outT(A_WIDTH), outT(C_QK_WIDTH),
                   pl.BlockSpec((1, tm, A_KV_WIDTH), lambda bi, i: (bi, i, 0)),
                   pl.BlockSpec((1, tm, C_QK_WIDTH), lambda bi, i: (bi, i, 0)),
                   outT(A_KV_WIDTH), outT(C_WIDTH), outT(REST_WIDTH)],
        out_shape=[jax.ShapeDtypeStruct((b, A_WIDTH, t), BF16),
                   jax.ShapeDtypeStruct((b, C_QK_WIDTH, t), BF16),
                   jax.ShapeDtypeStruct((b, t, A_KV_WIDTH), BF16),
                   jax.ShapeDtypeStruct((b, t, C_QK_WIDTH), BF16),
                   jax.ShapeDtypeStruct((b, A_KV_WIDTH, t), BF16),
                   jax.ShapeDtypeStruct((b, C_WIDTH, t), BF16),
                   jax.ShapeDtypeStruct((b, REST_WIDTH, t), BF16)],
        compiler_params=pltpu.CompilerParams(
            dimension_semantics=("parallel", "parallel"), vmem_limit_bytes=VMEM_LIMIT),
        name="in_proj",
    )(xT, shift, scale, g_norm, w_t, gqa, gka, gqc, gkc, cosa, sina, cosc, sinc)


def _attn_kernel(*refs, lam_init, online):
    if online:
        (lam_ref, gsub_ref, qa_ref, qc_ref, ka_ref, kc_ref, va_ref, vc_ref,
         oa_ref, oc_ref, acc_a, acc_c, m_a, m_c) = refs
    else:
        (stab_ref, lam_ref, gsub_ref, qa_ref, qc_ref, ka_ref, kc_ref, va_ref, vc_ref,
         oa_ref, oc_ref, lmin_ref, acc_a, acc_c) = refs
        m_a = m_c = None
    kv = pl.program_id(2)

    @pl.when(kv == 0)
    def _init():
        acc_a[...] = jnp.zeros_like(acc_a)
        acc_c[...] = jnp.zeros_like(acc_c)
        if online:
            m_a[...] = jnp.full_like(m_a, -1e30)
            m_c[...] = jnp.full_like(m_c, -1e30)

    def padded(q, u, n_units):
        z = jnp.zeros_like(q)
        return jnp.concatenate([q if j == u else z for j in range(n_units)], axis=0)

    def probs(acc_ref, m_ref, stab, u, k, qpad):
        s = jnp.dot(k, qpad, preferred_element_type=F32)
        if not online:
            return jnp.exp2(s - stab).astype(BF16), None
        m_prev = m_ref[u:u + 1, :]
        m_new = jnp.maximum(m_prev, jnp.max(s, axis=0, keepdims=True))
        m_ref[u:u + 1, :] = m_new
        return jnp.exp2(s - m_new).astype(BF16), jnp.exp2(m_prev - m_new)

    def accumulate(acc_ref, u, v_aug, p, alpha):
        pv = jnp.dot(v_aug, p, preferred_element_type=F32)
        acc_ref[u] = (acc_ref[u] + pv) if alpha is None else (acc_ref[u] * alpha + pv)

    stab_a, stab_c = (None, None) if online else (stab_ref[0], stab_ref[1])
    ka, kc = ka_ref[0], kc_ref[0]
    units = []
    for hd in range(A_HEADS):
        h = hd // A_GROUP
        q = qa_ref[0, hd * A_HEAD_DIM:(hd + 1) * A_HEAD_DIM, :]
        units.append(((acc_a, m_a, stab_a, hd, ka, padded(q, h, A_KV_HEADS)), (h, va_ref)))
    for u in range(2 * C_HEADS):
        q = qc_ref[0, u * C_QK_DIM:(u + 1) * C_QK_DIM, :]
        units.append(((acc_c, m_c, stab_c, u, kc, padded(q, u, 2 * C_HEADS)), (u // 2, vc_ref)))
    pending = None
    for qk_args, (vh, v_ref) in units:
        p, alpha = probs(*qk_args)
        if pending is not None:
            accumulate(*pending)
        pending = (qk_args[0], qk_args[3], v_ref[0, vh], p, alpha)
    accumulate(*pending)

    @pl.when(kv == pl.num_programs(2) - 1)
    def _finish():
        lmin = None
        for hd in range(A_HEADS):
            acc = acc_a[hd]
            l = acc[A_HEAD_DIM:A_HEAD_DIM + 1]
            lmin = l if lmin is None else jnp.minimum(lmin, l)
            oa_ref[0, hd * A_HEAD_DIM:(hd + 1) * A_HEAD_DIM, :] = (
                acc[:A_HEAD_DIM] / l).astype(BF16)
        lam_c = lam_ref[...]
        lam = (jnp.exp(jnp.sum(lam_c[0:1] * lam_c[1:2], axis=1, keepdims=True))
               - jnp.exp(jnp.sum(lam_c[2:3] * lam_c[3:4], axis=1, keepdims=True)) + lam_init)
        for h in range(C_HEADS):
            a1, a2 = acc_c[2 * h], acc_c[2 * h + 1]
            l1, l2 = a1[C_V_DIM:C_V_DIM + 1], a2[C_V_DIM:C_V_DIM + 1]
            lmin = jnp.minimum(lmin, jnp.minimum(l1, l2))
            o = a1[:C_V_DIM] / l1 - lam * (a2[:C_V_DIM] / l2)
            ms = jnp.mean(o * o, axis=0, keepdims=True)
            o = o * lax.rsqrt(ms + EPS) * gsub_ref[...] * (1.0 - lam_init)
            oc_ref[0, h * C_V_DIM:(h + 1) * C_V_DIM, :] = o.astype(BF16)
        if not online:
            lmin_ref[0, 0] = jnp.broadcast_to(lmin, lmin_ref.shape[2:])


def _attention(stab, lam_c, g_subln, qa, qc, ka, kc, va, vc, lam_init, online):
    b, _, t = qa.shape
    tkv = ka.shape[1]
    tq = _pick(t, (512, 256, 128))
    tk = _pick(tkv, (1280, 1024, 512, 256))
    nq = t // tq
    in_specs = [pl.BlockSpec((4, C_QK_DIM), lambda bi, i, j: (0, 0)),
                pl.BlockSpec((C_V_DIM, 1), lambda bi, i, j: (0, 0)),
                pl.BlockSpec((1, A_WIDTH, tq), lambda bi, i, j: (bi, 0, i)),
                pl.BlockSpec((1, C_QK_WIDTH, tq), lambda bi, i, j: (bi, 0, i)),
                pl.BlockSpec((1, tk, A_KV_WIDTH), lambda bi, i, j: (bi, j, 0)),
                pl.BlockSpec((1, tk, C_QK_WIDTH), lambda bi, i, j: (bi, j, 0)),
                pl.BlockSpec((1, A_KV_HEADS, V_ROWS, tk), lambda bi, i, j: (bi, 0, 0, j)),
                pl.BlockSpec((1, C_HEADS, V_ROWS, tk), lambda bi, i, j: (bi, 0, 0, j))]
    out_specs = [pl.BlockSpec((1, A_WIDTH, tq), lambda bi, i, j: (bi, 0, i)),
                 pl.BlockSpec((1, C_WIDTH, tq), lambda bi, i, j: (bi, 0, i))]
    out_shape = [jax.ShapeDtypeStruct((b, A_WIDTH, t), BF16),
                 jax.ShapeDtypeStruct((b, C_WIDTH, t), BF16)]
    scratch = [pltpu.VMEM((A_HEADS, V_ROWS, tq), F32), pltpu.VMEM((2 * C_HEADS, V_ROWS, tq), F32)]
    args = (lam_c, g_subln, qa, qc, ka, kc, va, vc)
    if online:
        scratch += [pltpu.VMEM((A_HEADS, tq), F32), pltpu.VMEM((2 * C_HEADS, tq), F32)]
    else:
        in_specs = [pl.BlockSpec(memory_space=pltpu.SMEM)] + in_specs
        out_specs.append(pl.BlockSpec((1, 1, 8, tq), lambda bi, i, j: (bi, i, 0, 0)))
        out_shape.append(jax.ShapeDtypeStruct((b, nq, 8, tq), F32))
        args = (stab,) + args
    return pl.pallas_call(
        functools.partial(_attn_kernel, lam_init=lam_init, online=online),
        grid=(b, nq, tkv // tk),
        in_specs=in_specs, out_specs=out_specs, out_shape=out_shape, scratch_shapes=scratch,
        compiler_params=pltpu.CompilerParams(
            dimension_semantics=("parallel", "parallel", "arbitrary"),
            vmem_limit_bytes=VMEM_LIMIT),
        name="attention_online" if online else "attention_fixed",
    )(*args)


def _attend(gains_a, gains_c, lam_c, g_subln, qa, qc, ka, kc, va, vc, lam_init):
    bound = lambda gains, d: (1.02 * LOG2E * d ** 0.5
                              * jnp.max(jnp.abs(gains[0])) * jnp.max(jnp.abs(gains[1])))
    stab = jnp.stack([bound(gains_a, A_HEAD_DIM), bound(gains_c, C_QK_DIM)]) - STAB_MARGIN
    args = (lam_c, g_subln, qa, qc, ka, kc, va, vc, lam_init)
    oa, oc, lmin = _attention(stab.astype(F32), *args, online=False)
    return lax.cond(jnp.all(lmin >= MIN_DENOM), lambda: (oa, oc),
                    lambda: tuple(_attention(None, *args, online=True)))


def _merge_kernel(x_ref, gate_ref, oa_ref, oc_ref, rest_ref, gsgu_ref, wsp_ref, bsp_ref,
                  wpa_ref, wpb_ref, wpc_ref, wout_ref, o_ref):
    tm = x_ref.shape[2]
    rest = lambda r0, n: rest_ref[0, r0:r0 + n, :].astype(F32)

    ha = (oa_ref[0].astype(F32) * _silu(rest(S_ZA, A_WIDTH))).astype(BF16)
    ya = jnp.dot(wpa_ref[...], ha, preferred_element_type=F32)
    hc = (oc_ref[0].astype(F32) * _silu(rest(S_ZC, C_WIDTH))).astype(BF16)
    yc = jnp.dot(wpc_ref[...], hc, preferred_element_type=F32)

    mixed_groups = []
    for g in range(B_GROUPS):
        v = rest(S_BV + g * B_GROUP_DIM, B_GROUP_DIM)
        ms = jnp.mean(v * v, axis=0, keepdims=True)
        vn = (v * lax.rsqrt(ms + EPS) * gsgu_ref[g]).astype(BF16)
        cols = [jnp.dot(vn[:, c:c + SGU_BD], wsp_ref[g], preferred_element_type=F32) + bsp_ref[g]
                for c in range(0, tm, SGU_BD)]
        mixed_groups.append(cols[0] if len(cols) == 1 else jnp.concatenate(cols, axis=1))
    mixed = jnp.concatenate(mixed_groups, axis=0)
    hb = (rest(S_BU, B_WIDTH) * mixed * _silu(rest(S_BZ, B_WIDTH))).astype(BF16)
    yb = jnp.dot(wpb_ref[...], hb, preferred_element_type=F32)

    m = (_sigmoid(rest(S_GATES, D_MODEL)) * ya
         + _sigmoid(rest(S_GATES + D_MODEL, D_MODEL)) * yb
         + _sigmoid(rest(S_GATES + 2 * D_MODEL, D_MODEL)) * yc)
    out = jnp.dot(wout_ref[...], m.astype(BF16), preferred_element_type=F32)
    o_ref[0] = x_ref[0] + gate_ref[0] * out


def _merge(xT, gate, oa, oc, rest, gsgu, wsp_bd, bsp, wpa_t, wpb_t, wpc_t, wout_t):
    b, _, t = xT.shape
    tm = _pick(t, (512, 256))
    tokT = lambda n: pl.BlockSpec((1, n, tm), lambda bi, i: (bi, 0, i))
    full = lambda shape: pl.BlockSpec(shape, lambda bi, i: (0,) * len(shape))
    return pl.pallas_call(
        _merge_kernel,
        grid=(b, t // tm),
        in_specs=[tokT(D_MODEL),
                  pl.BlockSpec((1, D_MODEL, 1), lambda bi, i: (bi, 0, 0)),
                  tokT(A_WIDTH), tokT(C_WIDTH), tokT(REST_WIDTH),
                  full((B_GROUPS, B_GROUP_DIM, 1)), full((B_GROUPS, SGU_BD, SGU_BD)),
                  full((B_GROUPS, 1, SGU_BD)),
                  full((D_MODEL, A_WIDTH)), full((D_MODEL, B_WIDTH)), full((D_MODEL, C_WIDTH)),
                  full((D_MODEL, D_MODEL))],
        out_specs=tokT(D_MODEL),
        out_shape=jax.ShapeDtypeStruct((b, D_MODEL, t), F32),
        compiler_params=pltpu.CompilerParams(
            dimension_semantics=("parallel", "parallel"), vmem_limit_bytes=VMEM_LIMIT),
        name="merge_out",
    )(xT, gate, oa, oc, rest, gsgu, wsp_bd, bsp, wpa_t, wpb_t, wpc_t, wout_t)


def _deinterleave(n_heads, dim):
    idx = np.arange(n_heads * dim).reshape(n_heads, dim // 2, 2)
    return np.concatenate([idx[:, :, 0], idx[:, :, 1]], axis=1).reshape(-1)


def _row_order():
    off = np.cumsum([0, A_WIDTH, C_QK_WIDTH, A_KV_WIDTH, A_KV_WIDTH, C_QK_WIDTH, C_WIDTH])
    qa = off[0] + _deinterleave(A_HEADS, A_HEAD_DIM)
    qc = off[1] + _deinterleave(2 * C_HEADS, C_QK_DIM)
    ka = off[2] + _deinterleave(A_KV_HEADS, A_HEAD_DIM)
    va = off[3] + np.arange(A_KV_WIDTH)
    kc = off[4] + _deinterleave(2 * C_HEADS, C_QK_DIM)
    vc = off[5] + np.arange(C_WIDTH)
    rest = np.arange(off[6], PROJ_WIDTH)
    return np.concatenate([qa, qc, ka, kc, va, vc, rest])


def _rope_tables(t, dim):
    quarter = dim // 4
    tok = jnp.arange(t)
    inv = jnp.power(ROPE_THETA, -jnp.arange(quarter, dtype=F32) / quarter)
    ang = jnp.concatenate([(tok // GRID_W)[:, None].astype(F32) * inv,
                           (tok % GRID_W)[:, None].astype(F32) * inv], axis=-1)
    return jnp.cos(ang).T, jnp.sin(ang).T


def _with_ones(vT, n_heads):
    b, _, t = vT.shape
    v = vT.reshape(b, n_heads, -1, t)
    ones = jnp.ones((b, n_heads, V_ROWS - v.shape[2], t), vT.dtype)
    return jnp.concatenate([v, ones], axis=2)


def kernel(x, c, ctx, c_ctx, w_mod, b_mod, g_norm, w_in, gq_a, gk_a, gq_c, gk_c, g_sgu, w_sp,
           b_sp, lam_c, g_subln, w_pa, w_pb, w_pc, w_out):
    depth = w_in.shape[0]
    b, t, _ = x.shape
    n_ctx = ctx.shape[1]

    cs = jnp.zeros((8, D_MODEL), F32).at[:b].set(c).at[b].set(c_ctx)
    mod = _modulation(cs, w_mod, b_mod)

    rows = _row_order()
    da, dc = _deinterleave(1, A_HEAD_DIM), _deinterleave(1, C_QK_DIM)
    cosa, sina = _rope_tables(t, A_HEAD_DIM)
    cosc, sinc = _rope_tables(t, C_QK_DIM)
    one = lambda n: jnp.ones((n, n_ctx), F32)
    zero = lambda n: jnp.zeros((n, n_ctx), F32)
    eye2 = jnp.eye(SGU_BD // B_CHUNK, dtype=F32)

    xT = jnp.transpose(x, (0, 2, 1))
    cT = jnp.transpose(ctx, (0, 2, 1))
    for l in range(depth):
        lam_init = 0.8 - 0.6 * math.exp(-0.3 * l)
        colv = lambda v: v.reshape(-1, D_MODEL, 1)
        shift, scale, gate = (colv(mod[l, :b, i * D_MODEL:(i + 1) * D_MODEL]) for i in range(3))
        shift_c, scale_c, gate_c = (
            colv(jnp.broadcast_to(mod[l, b, i * D_MODEL:(i + 1) * D_MODEL], (b, D_MODEL)))
            for i in range(3))
        w_t = w_in[l].T[rows].astype(BF16)
        gn = g_norm[l].reshape(D_MODEL, 1)
        gqa = (gq_a[l][da] * (A_HEAD_DIM ** -0.5 * LOG2E)).reshape(-1, 1)
        gka = gk_a[l][da].reshape(-1, 1)
        gqc = (gq_c[l][dc] * (C_QK_DIM ** -0.5 * LOG2E)).reshape(-1, 1)
        gkc = gk_c[l][dc].reshape(-1, 1)
        proj = functools.partial(_project, g_norm=gn, w_t=w_t, gqa=gqa, gka=gka, gqc=gqc, gkc=gkc)

        qa, qc, ka, kc, va, vc, rest = proj(
            xT, shift, scale, cosa=cosa, sina=sina, cosc=cosc, sinc=sinc)
        qa_x, qc_x, ka_x, kc_x, va_x, vc_x, rest_x = proj(
            cT, shift_c, scale_c, cosa=one(A_HEAD_DIM // 2), sina=zero(A_HEAD_DIM // 2),
            cosc=one(C_QK_DIM // 2), sinc=zero(C_QK_DIM // 2))

        attend = functools.partial(
            _attend, (gq_a[l], gk_a[l]), (gq_c[l], gk_c[l]), lam_c[l],
            g_subln[l].reshape(C_V_DIM, 1), lam_init=lam_init)
        va_x1, vc_x1 = _with_ones(va_x, A_KV_HEADS), _with_ones(vc_x, C_HEADS)
        oa, oc = attend(
            qa, qc,
            jnp.concatenate([ka_x, ka], axis=1), jnp.concatenate([kc_x, kc], axis=1),
            jnp.concatenate([va_x1, _with_ones(va, A_KV_HEADS)], axis=3),
            jnp.concatenate([vc_x1, _with_ones(vc, C_HEADS)], axis=3))

        wsp_bd = jnp.einsum("ij,gqp->giqjp", eye2, jnp.swapaxes(w_sp[l], 1, 2)).reshape(
            B_GROUPS, SGU_BD, SGU_BD).astype(BF16)
        bsp = jnp.tile(b_sp[l], (1, SGU_BD // B_CHUNK)).reshape(B_GROUPS, 1, SGU_BD)
        merge = functools.partial(
            _merge, gsgu=g_sgu[l].reshape(B_GROUPS, B_GROUP_DIM, 1), wsp_bd=wsp_bd, bsp=bsp,
            wpa_t=w_pa[l].T.astype(BF16), wpb_t=w_pb[l].T.astype(BF16),
            wpc_t=w_pc[l].T.astype(BF16), wout_t=w_out[l].T.astype(BF16))
        xT_new = merge(xT, gate, oa, oc, rest)
        if l < depth - 1:
            oa_x, oc_x = attend(qa_x, qc_x, ka_x, kc_x, va_x1, vc_x1)
            cT = merge(cT, gate_c, oa_x, oc_x, rest_x)
        xT = xT_new
    return jnp.transpose(xT, (0, 2, 1))
```

```python
import functools
import math

import numpy as np
import jax
import jax.numpy as jnp
from jax import lax
from jax.experimental import pallas as pl
from jax.experimental.pallas import tpu as pltpu

F32 = jnp.float32
BF16 = jnp.bfloat16

D_MODEL = 1024
GRID_W = 64
ROPE_THETA = 10000.0
EPS = 1e-6
N_BRANCH = 3

A_HEAD_DIM = 64
A_WIDTH = D_MODEL // 2
A_HEADS = A_WIDTH // A_HEAD_DIM
A_KV_HEADS = A_HEADS // 4
A_GROUP = A_HEADS // A_KV_HEADS
A_KV_WIDTH = A_KV_HEADS * A_HEAD_DIM

B_WIDTH = D_MODEL // 4
B_GROUP_DIM = 64
B_GROUPS = B_WIDTH // B_GROUP_DIM
B_CHUNK = 128

C_WIDTH = D_MODEL // 4
C_QK_DIM = 32
C_V_DIM = 2 * C_QK_DIM
C_HEADS = C_WIDTH // C_V_DIM
C_QK_WIDTH = C_HEADS * 2 * C_QK_DIM

PROJ_WIDTH = 6144
R_QA, R_QC, R_KA, R_KC, R_VA, R_VC, R_REST = 0, 512, 768, 896, 1152, 1280, 1536
REST_WIDTH = PROJ_WIDTH - R_REST
S_ZA, S_ZC, S_BU, S_BV, S_BZ, S_GATES = 0, 512, 768, 1024, 1280, 1536

ACC_ROWS = 72
KEY_CHUNK = 256
PIPELINE_DEPTH = 2
SGU_BD = 2 * B_CHUNK
VMEM_LIMIT = 56 * 1024 * 1024
LOG2E = math.log2(math.e)

MIN_DENOM = 2.0 ** -80
STAB_MARGIN = 40.0


def _pick(n, candidates):
    for c in candidates:
        if n % c == 0:
            return c
    raise ValueError(f"no tile for {n} in {candidates}")


def _silu(x):
    return x / (1.0 + jnp.exp(-x))


def _sigmoid(x):
    return 1.0 / (1.0 + jnp.exp(-x))


def _mod_kernel(c_ref, w_ref, b_ref, o_ref):
    s = _silu(c_ref[...])
    o_ref[0] = jnp.dot(s, w_ref[0], preferred_element_type=F32,
                       precision=lax.Precision.HIGHEST) + b_ref[0]


def _modulation(cs, w_mod, b_mod):
    depth = w_mod.shape[0]
    nj = 3 * D_MODEL // 1024
    return pl.pallas_call(
        _mod_kernel,
        grid=(depth, nj),
        in_specs=[pl.BlockSpec((8, D_MODEL), lambda l, j: (0, 0)),
                  pl.BlockSpec((1, D_MODEL, 1024), lambda l, j: (l, 0, j)),
                  pl.BlockSpec((1, 1, 1024), lambda l, j: (l, 0, j))],
        out_specs=pl.BlockSpec((1, 8, 1024), lambda l, j: (l, 0, j)),
        out_shape=jax.ShapeDtypeStruct((depth, 8, 3 * D_MODEL), F32),
        compiler_params=pltpu.CompilerParams(
            dimension_semantics=("parallel", "parallel"), vmem_limit_bytes=VMEM_LIMIT),
        name="adaln_mod",
    )(cs, w_mod, b_mod.reshape(depth, 1, 3 * D_MODEL))


def _norm_rope(p, g, cos, sin):
    half = p.shape[0] // 2
    ms = jnp.mean(p * p, axis=0, keepdims=True)
    y = p * lax.rsqrt(ms + EPS) * g
    x0, x1 = y[:half], y[half:]
    return jnp.concatenate([x0 * cos - x1 * sin, x0 * sin + x1 * cos], axis=0)


def _proj_kernel(x_ref, shift_ref, scale_ref, gn_ref, w_ref, gqa_ref, gka_ref, gqc_ref, gkc_ref,
                 cosa_ref, sina_ref, cosc_ref, sinc_ref,
                 qa_ref, qc_ref, ka_ref, kc_ref, va_ref, vc_ref, rest_ref):
    x = x_ref[0]
    ms = jnp.mean(x * x, axis=0, keepdims=True)
    y = x * lax.rsqrt(ms + EPS) * gn_ref[...]
    hn = (y * (1.0 + scale_ref[0]) + shift_ref[0]).astype(BF16)

    def proj(r0, r1):
        return jnp.dot(w_ref[r0:r1, :], hn, preferred_element_type=F32)

    cosa, sina, cosc, sinc = cosa_ref[...], sina_ref[...], cosc_ref[...], sinc_ref[...]

    pqa = proj(R_QA, R_QC)
    for h in range(A_HEADS):
        r = h * A_HEAD_DIM
        qa_ref[0, r:r + A_HEAD_DIM, :] = _norm_rope(
            pqa[r:r + A_HEAD_DIM], gqa_ref[...], cosa, sina).astype(BF16)
    pqc = proj(R_QC, R_KA)
    for u in range(2 * C_HEADS):
        r = u * C_QK_DIM
        qc_ref[0, r:r + C_QK_DIM, :] = _norm_rope(
            pqc[r:r + C_QK_DIM], gqc_ref[...], cosc, sinc).astype(BF16)
    pka = proj(R_KA, R_KC)
    ka = jnp.concatenate(
        [_norm_rope(pka[h * A_HEAD_DIM:(h + 1) * A_HEAD_DIM], gka_ref[...], cosa, sina)
         for h in range(A_KV_HEADS)], axis=0)
    ka_ref[0] = ka.T.astype(BF16)
    pkc = proj(R_KC, R_VA)
    kc = jnp.concatenate(
        [_norm_rope(pkc[u * C_QK_DIM:(u + 1) * C_QK_DIM], gkc_ref[...], cosc, sinc)
         for u in range(2 * C_HEADS)], axis=0)
    kc_ref[0] = kc.T.astype(BF16)
    va_ref[0] = proj(R_VA, R_VC).astype(BF16)
    vc_ref[0] = proj(R_VC, R_REST).astype(BF16)
    for r in range(R_REST, PROJ_WIDTH, 512):
        rest_ref[0, r - R_REST:r - R_REST + 512, :] = proj(r, r + 512).astype(BF16)


def _project(xT, shift, scale, g_norm, w_t, gqa, gka, gqc, gkc, cosa, sina, cosc, sinc):
    b, _, t = xT.shape
    tm = _pick(t, (512, 256, 128))
    col = lambda n: pl.BlockSpec((n, 1), lambda bi, i: (0, 0))
    tok = lambda n: pl.BlockSpec((n, tm), lambda bi, i: (0, i))
    outT = lambda n: pl.BlockSpec((1, n, tm), lambda bi, i: (bi, 0, i))
    return pl.pallas_call(
        _proj_kernel,
        grid=(b, t // tm),
        in_specs=[pl.BlockSpec((1, D_MODEL, tm), lambda bi, i: (bi, 0, i)),
                  pl.BlockSpec((1, D_MODEL, 1), lambda bi, i: (bi, 0, 0)),
                  pl.BlockSpec((1, D_MODEL, 1), lambda bi, i: (bi, 0, 0)),
                  col(D_MODEL),
                  pl.BlockSpec((PROJ_WIDTH, D_MODEL), lambda bi, i: (0, 0),
                               pipeline_mode=pl.Buffered(1)),
                  col(A_HEAD_DIM), col(A_HEAD_DIM), col(C_QK_DIM), col(C_QK_DIM),
                  tok(A_HEAD_DIM // 2), tok(A_HEAD_DIM // 2), tok(C_QK_DIM // 2), tok(C_QK_DIM // 2)],
        out_specs=[outT(A_WIDTH), outT(C_QK_WIDTH),
                   pl.BlockSpec((1, tm, A_KV_WIDTH), lambda bi, i: (bi, i, 0)),
                   pl.BlockSpec((1, tm, C_QK_WIDTH), lambda bi, i: (bi, i, 0)),
                   outT(A_KV_WIDTH), outT(C_WIDTH), outT(REST_WIDTH)],
        out_shape=[jax.ShapeDtypeStruct((b, A_WIDTH, t), BF16),
                   jax.ShapeDtypeStruct((b, C_QK_WIDTH, t), BF16),
                   jax.ShapeDtypeStruct((b, t, A_KV_WIDTH), BF16),
                   jax.ShapeDtypeStruct((b, t, C_QK_WIDTH), BF16),
                   jax.ShapeDtypeStruct((b, A_KV_WIDTH, t), BF16),
                   jax.ShapeDtypeStruct((b, C_WIDTH, t), BF16),
                   jax.ShapeDtypeStruct((b, REST_WIDTH, t), BF16)],
        compiler_params=pltpu.CompilerParams(
            dimension_semantics=("parallel", "parallel"), vmem_limit_bytes=VMEM_LIMIT),
        name="in_proj",
    )(xT, shift, scale, g_norm, w_t, gqa, gka, gqc, gkc, cosa, sina, cosc, sinc)


def _attn_kernel(*refs, lam_init, online):
    if online:
        (lam_ref, gsub_ref, qa_ref, qc_ref, ka_ref, kc_ref, va_ref, vc_ref,
         oa_ref, oc_ref, acc_a, acc_c, m_a, m_c) = refs
    else:
        (stab_ref, lam_ref, gsub_ref, qa_ref, qc_ref, ka_ref, kc_ref, va_ref, vc_ref,
         oa_ref, oc_ref, lmin_ref, acc_a, acc_c) = refs
        m_a = m_c = None
    kv = pl.program_id(2)

    @pl.when(kv == 0)
    def _init():
        acc_a[...] = jnp.zeros_like(acc_a)
        acc_c[...] = jnp.zeros_like(acc_c)
        if online:
            m_a[...] = jnp.full_like(m_a, -1e30)
            m_c[...] = jnp.full_like(m_c, -1e30)

    def padded(q, u, n_units):
        z = jnp.zeros_like(q)
        return jnp.concatenate([q if j == u else z for j in range(n_units)], axis=0)

    def probs(s, shift):
        pf = jnp.exp2(s - shift)
        return pf.astype(BF16), pf.reshape(s.shape[0] // 8, 8, s.shape[1]).sum(axis=0)

    def weighted(v, p, lsum):
        return jnp.concatenate([jnp.dot(v, p, preferred_element_type=F32), lsum], axis=0)

    tk = ka_ref.shape[1]
    units = []
    for hd in range(A_HEADS):
        h = hd // A_GROUP
        q = qa_ref[0, hd * A_HEAD_DIM:(hd + 1) * A_HEAD_DIM, :]
        units.append((acc_a, hd, ka_ref, padded(q, h, A_KV_HEADS), h, va_ref))
    for u in range(2 * C_HEADS):
        q = qc_ref[0, u * C_QK_DIM:(u + 1) * C_QK_DIM, :]
        units.append((acc_c, u, kc_ref, padded(q, u, 2 * C_HEADS), u // 2, vc_ref))

    if online:
        for acc_ref, u, k_ref, qpad, vh, v_ref in units:
            m_ref = m_a if acc_ref is acc_a else m_c
            s = jnp.dot(k_ref[0], qpad, preferred_element_type=F32)
            m_prev = m_ref[u:u + 1, :]
            m_new = jnp.maximum(m_prev, jnp.max(s, axis=0, keepdims=True))
            m_ref[u:u + 1, :] = m_new
            acc_ref[u] = (acc_ref[u] * jnp.exp2(m_prev - m_new)
                          + weighted(v_ref[0, vh], *probs(s, m_new)))
    else:
        n_chunks = tk // KEY_CHUNK
        pending, partial = [], {}

        def retire():
            ui, ci, p, lsum = pending.pop(0)
            acc_ref, u, _, _, vh, v_ref = units[ui]
            pv = weighted(v_ref[0, vh, :, ci * KEY_CHUNK:(ci + 1) * KEY_CHUNK], p, lsum)
            partial[ui] = pv if ci == 0 else partial[ui] + pv
            if ci == n_chunks - 1:
                acc_ref[u] += partial.pop(ui)

        for ui, (acc_ref, u, k_ref, qpad, vh, v_ref) in enumerate(units):
            stab = stab_ref[0] if acc_ref is acc_a else stab_ref[1]
            for ci in range(n_chunks):
                s = jnp.dot(k_ref[0, ci * KEY_CHUNK:(ci + 1) * KEY_CHUNK, :], qpad,
                            preferred_element_type=F32)
                pending.append((ui, ci) + probs(s, stab))
                if len(pending) > PIPELINE_DEPTH:
                    retire()
        while pending:
            retire()

    @pl.when(kv == pl.num_programs(2) - 1)
    def _finish():
        def normalised(acc, d):
            l = jnp.sum(acc[d:], axis=0, keepdims=True)
            return acc[:d] / l, l

        lmin = None
        for hd in range(A_HEADS):
            o, l = normalised(acc_a[hd], A_HEAD_DIM)
            lmin = l if lmin is None else jnp.minimum(lmin, l)
            oa_ref[0, hd * A_HEAD_DIM:(hd + 1) * A_HEAD_DIM, :] = o.astype(BF16)
        lam_c = lam_ref[...]
        lam = (jnp.exp(jnp.sum(lam_c[0:1] * lam_c[1:2], axis=1, keepdims=True))
               - jnp.exp(jnp.sum(lam_c[2:3] * lam_c[3:4], axis=1, keepdims=True)) + lam_init)
        for h in range(C_HEADS):
            (o1, l1), (o2, l2) = (normalised(acc_c[2 * h + j], C_V_DIM) for j in range(2))
            lmin = jnp.minimum(lmin, jnp.minimum(l1, l2))
            o = o1 - lam * o2
            ms = jnp.mean(o * o, axis=0, keepdims=True)
            o = o * lax.rsqrt(ms + EPS) * gsub_ref[...] * (1.0 - lam_init)
            oc_ref[0, h * C_V_DIM:(h + 1) * C_V_DIM, :] = o.astype(BF16)
        if not online:
            lmin_ref[0, 0] = jnp.broadcast_to(lmin, lmin_ref.shape[2:])


def _attention(stab, lam_c, g_subln, qa, qc, ka, kc, va, vc, lam_init, online):
    b, _, t = qa.shape
    tkv = ka.shape[1]
    if online:
        tq, tk = _pick(t, (256, 128)), _pick(tkv, (1280, 1024, 512, 256))
    else:
        tq = _pick(t, (512, 256, 128))
        tk = _pick(tkv, (3328, 1280, 1024, 512, 256))
    nq = t // tq
    in_specs = [pl.BlockSpec((4, C_QK_DIM), lambda bi, i, j: (0, 0)),
                pl.BlockSpec((C_V_DIM, 1), lambda bi, i, j: (0, 0)),
                pl.BlockSpec((1, A_WIDTH, tq), lambda bi, i, j: (bi, 0, i)),
                pl.BlockSpec((1, C_QK_WIDTH, tq), lambda bi, i, j: (bi, 0, i)),
                pl.BlockSpec((1, tk, A_KV_WIDTH), lambda bi, i, j: (bi, j, 0)),
                pl.BlockSpec((1, tk, C_QK_WIDTH), lambda bi, i, j: (bi, j, 0)),
                pl.BlockSpec((1, A_KV_HEADS, A_HEAD_DIM, tk), lambda bi, i, j: (bi, 0, 0, j)),
                pl.BlockSpec((1, C_HEADS, C_V_DIM, tk), lambda bi, i, j: (bi, 0, 0, j))]
    out_specs = [pl.BlockSpec((1, A_WIDTH, tq), lambda bi, i, j: (bi, 0, i)),
                 pl.BlockSpec((1, C_WIDTH, tq), lambda bi, i, j: (bi, 0, i))]
    out_shape = [jax.ShapeDtypeStruct((b, A_WIDTH, t), BF16),
                 jax.ShapeDtypeStruct((b, C_WIDTH, t), BF16)]
    scratch = [pltpu.VMEM((A_HEADS, ACC_ROWS, tq), F32),
               pltpu.VMEM((2 * C_HEADS, ACC_ROWS, tq), F32)]
    args = (lam_c, g_subln, qa, qc, ka, kc, va, vc)
    if online:
        scratch += [pltpu.VMEM((A_HEADS, tq), F32), pltpu.VMEM((2 * C_HEADS, tq), F32)]
    else:
        in_specs = [pl.BlockSpec(memory_space=pltpu.SMEM)] + in_specs
        out_specs.append(pl.BlockSpec((1, 1, 8, tq), lambda bi, i, j: (bi, i, 0, 0)))
        out_shape.append(jax.ShapeDtypeStruct((b, nq, 8, tq), F32))
        args = (stab,) + args
    return pl.pallas_call(
        functools.partial(_attn_kernel, lam_init=lam_init, online=online),
        grid=(b, nq, tkv // tk),
        in_specs=in_specs, out_specs=out_specs, out_shape=out_shape, scratch_shapes=scratch,
        compiler_params=pltpu.CompilerParams(
            dimension_semantics=("parallel", "parallel", "arbitrary"),
            vmem_limit_bytes=VMEM_LIMIT),
        name="attention_online" if online else "attention_fixed",
    )(*args)


def _attend(gains_a, gains_c, lam_c, g_subln, qa, qc, ka, kc, va, vc, lam_init):
    bound = lambda gains, d: (1.02 * LOG2E * d ** 0.5
                              * jnp.max(jnp.abs(gains[0])) * jnp.max(jnp.abs(gains[1])))
    stab = jnp.stack([bound(gains_a, A_HEAD_DIM), bound(gains_c, C_QK_DIM)]) - STAB_MARGIN
    args = (lam_c, g_subln, qa, qc, ka, kc, va, vc, lam_init)
    oa, oc, lmin = _attention(stab.astype(F32), *args, online=False)
    return lax.cond(jnp.all(lmin >= MIN_DENOM), lambda: (oa, oc),
                    lambda: tuple(_attention(None, *args, online=True)))


def _merge_kernel(x_ref, gate_ref, oa_ref, oc_ref, rest_ref, gsgu_ref, wsp_ref, bsp_ref,
                  wpa_ref, wpb_ref, wpc_ref, wout_ref, o_ref):
    tm = x_ref.shape[2]
    rest = lambda r0, n: rest_ref[0, r0:r0 + n, :].astype(F32)

    ha = (oa_ref[0].astype(F32) * _silu(rest(S_ZA, A_WIDTH))).astype(BF16)
    ya = jnp.dot(wpa_ref[...], ha, preferred_element_type=F32)
    hc = (oc_ref[0].astype(F32) * _silu(rest(S_ZC, C_WIDTH))).astype(BF16)
    yc = jnp.dot(wpc_ref[...], hc, preferred_element_type=F32)

    mixed_groups = []
    for g in range(B_GROUPS):
        v = rest(S_BV + g * B_GROUP_DIM, B_GROUP_DIM)
        ms = jnp.mean(v * v, axis=0, keepdims=True)
        vn = (v * lax.rsqrt(ms + EPS) * gsgu_ref[g]).astype(BF16)
        cols = [jnp.dot(vn[:, c:c + SGU_BD], wsp_ref[g], preferred_element_type=F32) + bsp_ref[g]
                for c in range(0, tm, SGU_BD)]
        mixed_groups.append(cols[0] if len(cols) == 1 else jnp.concatenate(cols, axis=1))
    mixed = jnp.concatenate(mixed_groups, axis=0)
    hb = (rest(S_BU, B_WIDTH) * mixed * _silu(rest(S_BZ, B_WIDTH))).astype(BF16)
    yb = jnp.dot(wpb_ref[...], hb, preferred_element_type=F32)

    m = (_sigmoid(rest(S_GATES, D_MODEL)) * ya
         + _sigmoid(rest(S_GATES + D_MODEL, D_MODEL)) * yb
         + _sigmoid(rest(S_GATES + 2 * D_MODEL, D_MODEL)) * yc)
    out = jnp.dot(wout_ref[...], m.astype(BF16), preferred_element_type=F32)
    o_ref[0] = x_ref[0] + gate_ref[0] * out


def _merge(xT, gate, oa, oc, rest, gsgu, wsp_bd, bsp, wpa_t, wpb_t, wpc_t, wout_t):
    b, _, t = xT.shape
    tm = _pick(t, (512, 256))
    tokT = lambda n: pl.BlockSpec((1, n, tm), lambda bi, i: (bi, 0, i))
    full = lambda shape: pl.BlockSpec(shape, lambda bi, i: (0,) * len(shape))
    return pl.pallas_call(
        _merge_kernel,
        grid=(b, t // tm),
        in_specs=[tokT(D_MODEL),
                  pl.BlockSpec((1, D_MODEL, 1), lambda bi, i: (bi, 0, 0)),
                  tokT(A_WIDTH), tokT(C_WIDTH), tokT(REST_WIDTH),
                  full((B_GROUPS, B_GROUP_DIM, 1)), full((B_GROUPS, SGU_BD, SGU_BD)),
                  full((B_GROUPS, 1, SGU_BD)),
                  full((D_MODEL, A_WIDTH)), full((D_MODEL, B_WIDTH)), full((D_MODEL, C_WIDTH)),
                  full((D_MODEL, D_MODEL))],
        out_specs=tokT(D_MODEL),
        out_shape=jax.ShapeDtypeStruct((b, D_MODEL, t), F32),
        compiler_params=pltpu.CompilerParams(
            dimension_semantics=("parallel", "parallel"), vmem_limit_bytes=VMEM_LIMIT),
        name="merge_out",
    )(xT, gate, oa, oc, rest, gsgu, wsp_bd, bsp, wpa_t, wpb_t, wpc_t, wout_t)


def _deinterleave(n_heads, dim):
    idx = np.arange(n_heads * dim).reshape(n_heads, dim // 2, 2)
    return np.concatenate([idx[:, :, 0], idx[:, :, 1]], axis=1).reshape(-1)


def _row_order():
    off = np.cumsum([0, A_WIDTH, C_QK_WIDTH, A_KV_WIDTH, A_KV_WIDTH, C_QK_WIDTH, C_WIDTH])
    qa = off[0] + _deinterleave(A_HEADS, A_HEAD_DIM)
    qc = off[1] + _deinterleave(2 * C_HEADS, C_QK_DIM)
    ka = off[2] + _deinterleave(A_KV_HEADS, A_HEAD_DIM)
    va = off[3] + np.arange(A_KV_WIDTH)
    kc = off[4] + _deinterleave(2 * C_HEADS, C_QK_DIM)
    vc = off[5] + np.arange(C_WIDTH)
    rest = np.arange(off[6], PROJ_WIDTH)
    return np.concatenate([qa, qc, ka, kc, va, vc, rest])


def _rope_tables(t, dim):
    quarter = dim // 4
    tok = jnp.arange(t)
    inv = jnp.power(ROPE_THETA, -jnp.arange(quarter, dtype=F32) / quarter)
    ang = jnp.concatenate([(tok // GRID_W)[:, None].astype(F32) * inv,
                           (tok % GRID_W)[:, None].astype(F32) * inv], axis=-1)
    return jnp.cos(ang).T, jnp.sin(ang).T


def kernel(x, c, ctx, c_ctx, w_mod, b_mod, g_norm, w_in, gq_a, gk_a, gq_c, gk_c, g_sgu, w_sp,
           b_sp, lam_c, g_subln, w_pa, w_pb, w_pc, w_out):
    depth = w_in.shape[0]
    b, t, _ = x.shape
    n_ctx = ctx.shape[1]

    cs = jnp.zeros((8, D_MODEL), F32).at[:b].set(c).at[b].set(c_ctx)
    mod = _modulation(cs, w_mod, b_mod)

    rows = _row_order()
    da, dc = _deinterleave(1, A_HEAD_DIM), _deinterleave(1, C_QK_DIM)
    cosa, sina = _rope_tables(t, A_HEAD_DIM)
    cosc, sinc = _rope_tables(t, C_QK_DIM)
    one = lambda n: jnp.ones((n, n_ctx), F32)
    zero = lambda n: jnp.zeros((n, n_ctx), F32)
    eye2 = jnp.eye(SGU_BD // B_CHUNK, dtype=F32)

    xT = jnp.transpose(x, (0, 2, 1))
    cT = jnp.transpose(ctx, (0, 2, 1))
    for l in range(depth):
        lam_init = 0.8 - 0.6 * math.exp(-0.3 * l)
        colv = lambda v: v.reshape(-1, D_MODEL, 1)
        shift, scale, gate = (colv(mod[l, :b, i * D_MODEL:(i + 1) * D_MODEL]) for i in range(3))
        shift_c, scale_c, gate_c = (
            colv(jnp.broadcast_to(mod[l, b, i * D_MODEL:(i + 1) * D_MODEL], (b, D_MODEL)))
            for i in range(3))
        w_t = w_in[l].T[rows].astype(BF16)
        gn = g_norm[l].reshape(D_MODEL, 1)
        gqa = (gq_a[l][da] * (A_HEAD_DIM ** -0.5 * LOG2E)).reshape(-1, 1)
        gka = gk_a[l][da].reshape(-1, 1)
        gqc = (gq_c[l][dc] * (C_QK_DIM ** -0.5 * LOG2E)).reshape(-1, 1)
        gkc = gk_c[l][dc].reshape(-1, 1)
        proj = functools.partial(_project, g_norm=gn, w_t=w_t, gqa=gqa, gka=gka, gqc=gqc, gkc=gkc)

        qa, qc, ka, kc, va, vc, rest = proj(
            xT, shift, scale, cosa=cosa, sina=sina, cosc=cosc, sinc=sinc)
        qa_x, qc_x, ka_x, kc_x, va_x, vc_x, rest_x = proj(
            cT, shift_c, scale_c, cosa=one(A_HEAD_DIM // 2), sina=zero(A_HEAD_DIM // 2),
            cosc=one(C_QK_DIM // 2), sinc=zero(C_QK_DIM // 2))

        attend = functools.partial(
            _attend, (gq_a[l], gk_a[l]), (gq_c[l], gk_c[l]), lam_c[l],
            g_subln[l].reshape(C_V_DIM, 1), lam_init=lam_init)
        heads = lambda vT, n: vT.reshape(b, n, -1, vT.shape[2])
        va_x, vc_x = heads(va_x, A_KV_HEADS), heads(vc_x, C_HEADS)
        oa, oc = attend(
            qa, qc,
            jnp.concatenate([ka_x, ka], axis=1), jnp.concatenate([kc_x, kc], axis=1),
            jnp.concatenate([va_x, heads(va, A_KV_HEADS)], axis=3),
            jnp.concatenate([vc_x, heads(vc, C_HEADS)], axis=3))

        wsp_bd = jnp.einsum("ij,gqp->giqjp", eye2, jnp.swapaxes(w_sp[l], 1, 2)).reshape(
            B_GROUPS, SGU_BD, SGU_BD).astype(BF16)
        bsp = jnp.tile(b_sp[l], (1, SGU_BD // B_CHUNK)).reshape(B_GROUPS, 1, SGU_BD)
        merge = functools.partial(
            _merge, gsgu=g_sgu[l].reshape(B_GROUPS, B_GROUP_DIM, 1), wsp_bd=wsp_bd, bsp=bsp,
            wpa_t=w_pa[l].T.astype(BF16), wpb_t=w_pb[l].T.astype(BF16),
            wpc_t=w_pc[l].T.astype(BF16), wout_t=w_out[l].T.astype(BF16))
        xT_new = merge(xT, gate, oa, oc, rest)
        if l < depth - 1:
            oa_x, oc_x = attend(qa_x, qc_x, ka_x, kc_x, va_x, vc_x)
            cT = merge(cT, gate_c, oa_x, oc_x, rest_x)
        xT = xT_new
    return jnp.transpose(xT, (0, 2, 1))
```

```python
import functools
import math

import numpy as np
import jax
import jax.numpy as jnp
from jax import lax
from jax.experimental import pallas as pl
from jax.experimental.pallas import tpu as pltpu

F32 = jnp.float32
BF16 = jnp.bfloat16

D_MODEL = 1024
GRID_W = 64
ROPE_THETA = 10000.0
EPS = 1e-6
N_BRANCH = 3

A_HEAD_DIM = 64
A_WIDTH = D_MODEL // 2
A_HEADS = A_WIDTH // A_HEAD_DIM
A_KV_HEADS = A_HEADS // 4
A_GROUP = A_HEADS // A_KV_HEADS
A_KV_WIDTH = A_KV_HEADS * A_HEAD_DIM

B_WIDTH = D_MODEL // 4
B_GROUP_DIM = 64
B_GROUPS = B_WIDTH // B_GROUP_DIM
B_CHUNK = 128

C_WIDTH = D_MODEL // 4
C_QK_DIM = 32
C_V_DIM = 2 * C_QK_DIM
C_HEADS = C_WIDTH // C_V_DIM
C_QK_WIDTH = C_HEADS * 2 * C_QK_DIM

PROJ_WIDTH = 6144
R_QA, R_QC, R_KA, R_KC, R_VA, R_VC, R_REST = 0, 512, 768, 896, 1152, 1280, 1536
REST_WIDTH = PROJ_WIDTH - R_REST
S_ZA, S_ZC, S_BU, S_BV, S_BZ, S_GATES = 0, 512, 768, 1024, 1280, 1536

ACC_ROWS = 72
KEY_CHUNK = 256
PIPELINE_DEPTH = 2
SGU_BD = 2 * B_CHUNK
VMEM_LIMIT = 56 * 1024 * 1024
LOG2E = math.log2(math.e)

MIN_DENOM = 2.0 ** -80
STAB_MARGIN = 40.0


def _pick(n, candidates):
    for c in candidates:
        if n % c == 0:
            return c
    raise ValueError(f"no tile for {n} in {candidates}")


def _sigmoid(x):
    return 0.5 * jnp.tanh(0.5 * x) + 0.5


def _silu(x):
    return x * _sigmoid(x)


def _mod_kernel(c_ref, w_ref, b_ref, o_ref):
    s = _silu(c_ref[...])
    o_ref[0] = jnp.dot(s, w_ref[0], preferred_element_type=F32,
                       precision=lax.Precision.HIGHEST) + b_ref[0]


def _modulation(cs, w_mod, b_mod):
    depth = w_mod.shape[0]
    nj = 3 * D_MODEL // 1024
    return pl.pallas_call(
        _mod_kernel,
        grid=(depth, nj),
        in_specs=[pl.BlockSpec((8, D_MODEL), lambda l, j: (0, 0)),
                  pl.BlockSpec((1, D_MODEL, 1024), lambda l, j: (l, 0, j)),
                  pl.BlockSpec((1, 1, 1024), lambda l, j: (l, 0, j))],
        out_specs=pl.BlockSpec((1, 8, 1024), lambda l, j: (l, 0, j)),
        out_shape=jax.ShapeDtypeStruct((depth, 8, 3 * D_MODEL), F32),
        compiler_params=pltpu.CompilerParams(
            dimension_semantics=("parallel", "parallel"), vmem_limit_bytes=VMEM_LIMIT),
        name="adaln_mod",
    )(cs, w_mod, b_mod.reshape(depth, 1, 3 * D_MODEL))


def _norm_rope(p, g, cos, sin):
    half = p.shape[0] // 2
    ms = jnp.mean(p * p, axis=0, keepdims=True)
    y = p * lax.rsqrt(ms + EPS) * g
    x0, x1 = y[:half], y[half:]
    return jnp.concatenate([x0 * cos - x1 * sin, x0 * sin + x1 * cos], axis=0)


def _proj_kernel(x_ref, shift_ref, scale_ref, gn_ref, w_ref, gqa_ref, gka_ref, gqc_ref, gkc_ref,
                 cosa_ref, sina_ref, cosc_ref, sinc_ref,
                 qa_ref, qc_ref, ka_ref, kc_ref, va_ref, vc_ref, rest_ref, hn_ref):
    x = x_ref[0]
    ms = jnp.mean(x * x, axis=1, keepdims=True)
    y = x * lax.rsqrt(ms + EPS) * gn_ref[...]
    hn_ref[...] = (y * (1.0 + scale_ref[0]) + shift_ref[0]).T.astype(BF16)

    def proj(r0, r1):
        return jnp.dot(w_ref[r0:r1, :], hn_ref[...], preferred_element_type=F32)

    cosa, sina, cosc, sinc = cosa_ref[...], sina_ref[...], cosc_ref[...], sinc_ref[...]

    pqa = proj(R_QA, R_QC)
    for h in range(A_HEADS):
        r = h * A_HEAD_DIM
        qa_ref[0, r:r + A_HEAD_DIM, :] = _norm_rope(
            pqa[r:r + A_HEAD_DIM], gqa_ref[...], cosa, sina).astype(BF16)
    pqc = proj(R_QC, R_KA)
    for u in range(2 * C_HEADS):
        r = u * C_QK_DIM
        qc_ref[0, r:r + C_QK_DIM, :] = _norm_rope(
            pqc[r:r + C_QK_DIM], gqc_ref[...], cosc, sinc).astype(BF16)
    pka = proj(R_KA, R_KC)
    ka = jnp.concatenate(
        [_norm_rope(pka[h * A_HEAD_DIM:(h + 1) * A_HEAD_DIM], gka_ref[...], cosa, sina)
         for h in range(A_KV_HEADS)], axis=0)
    ka_ref[0] = ka.T.astype(BF16)
    pkc = proj(R_KC, R_VA)
    kc = jnp.concatenate(
        [_norm_rope(pkc[u * C_QK_DIM:(u + 1) * C_QK_DIM], gkc_ref[...], cosc, sinc)
         for u in range(2 * C_HEADS)], axis=0)
    kc_ref[0] = kc.T.astype(BF16)
    va_ref[0] = proj(R_VA, R_VC).astype(BF16)
    vc_ref[0] = proj(R_VC, R_REST).astype(BF16)
    for r in range(R_REST, PROJ_WIDTH, 512):
        rest_ref[0, r - R_REST:r - R_REST + 512, :] = proj(r, r + 512).astype(BF16)


def _project(x, shift, scale, g_norm, w_t, gqa, gka, gqc, gkc, cosa, sina, cosc, sinc):
    b, t, _ = x.shape
    tm = _pick(t, (512, 256, 128))
    col = lambda n: pl.BlockSpec((n, 1), lambda bi, i: (0, 0))
    tok = lambda n: pl.BlockSpec((n, tm), lambda bi, i: (0, i))
    outT = lambda n: pl.BlockSpec((1, n, tm), lambda bi, i: (bi, 0, i))
    return pl.pallas_call(
        _proj_kernel,
        grid=(b, t // tm),
        in_specs=[pl.BlockSpec((1, tm, D_MODEL), lambda bi, i: (bi, i, 0)),
                  pl.BlockSpec((1, 1, D_MODEL), lambda bi, i: (bi, 0, 0)),
                  pl.BlockSpec((1, 1, D_MODEL), lambda bi, i: (bi, 0, 0)),
                  pl.BlockSpec((1, D_MODEL), lambda bi, i: (0, 0)),
                  pl.BlockSpec((PROJ_WIDTH, D_MODEL), lambda bi, i: (0, 0),
                               pipeline_mode=pl.Buffered(1)),
                  col(A_HEAD_DIM), col(A_HEAD_DIM), col(C_QK_DIM), col(C_QK_DIM),
                  tok(A_HEAD_DIM // 2), tok(A_HEAD_DIM // 2), tok(C_QK_DIM // 2), tok(C_QK_DIM // 2)],
        out_specs=[outT(A_WIDTH), outT(C_QK_WIDTH),
                   pl.BlockSpec((1, tm, A_KV_WIDTH), lambda bi, i: (bi, i, 0)),
                   pl.BlockSpec((1, tm, C_QK_WIDTH), lambda bi, i: (bi, i, 0)),
                   outT(A_KV_WIDTH), outT(C_WIDTH), outT(REST_WIDTH)],
        out_shape=[jax.ShapeDtypeStruct((b, A_WIDTH, t), BF16),
                   jax.ShapeDtypeStruct((b, C_QK_WIDTH, t), BF16),
                   jax.ShapeDtypeStruct((b, t, A_KV_WIDTH), BF16),
                   jax.ShapeDtypeStruct((b, t, C_QK_WIDTH), BF16),
                   jax.ShapeDtypeStruct((b, A_KV_WIDTH, t), BF16),
                   jax.ShapeDtypeStruct((b, C_WIDTH, t), BF16),
                   jax.ShapeDtypeStruct((b, REST_WIDTH, t), BF16)],
        scratch_shapes=[pltpu.VMEM((D_MODEL, tm), BF16)],
        compiler_params=pltpu.CompilerParams(
            dimension_semantics=("parallel", "parallel"), vmem_limit_bytes=VMEM_LIMIT),
        name="in_proj",
    )(x, shift, scale, g_norm, w_t, gqa, gka, gqc, gkc, cosa, sina, cosc, sinc)


def _attn_kernel(*refs, lam_init, online):
    if online:
        (lam_ref, gsub_ref, qa_ref, qc_ref, ka_ref, kc_ref, va_ref, vc_ref,
         oa_ref, oc_ref, acc_a, acc_c, m_a, m_c) = refs
    else:
        (stab_ref, lam_ref, gsub_ref, qa_ref, qc_ref, ka_ref, kc_ref, va_ref, vc_ref,
         oa_ref, oc_ref, lmin_ref, acc_a, acc_c) = refs
        m_a = m_c = None
    kv = pl.program_id(2)

    @pl.when(kv == 0)
    def _init():
        acc_a[...] = jnp.zeros_like(acc_a)
        acc_c[...] = jnp.zeros_like(acc_c)
        if online:
            m_a[...] = jnp.full_like(m_a, -1e30)
            m_c[...] = jnp.full_like(m_c, -1e30)

    def padded(q, u, n_units):
        z = jnp.zeros_like(q)
        return jnp.concatenate([q if j == u else z for j in range(n_units)], axis=0)

    def probs(s, shift):
        pf = jnp.exp2(s - shift)
        return pf.astype(BF16), pf.reshape(s.shape[0] // 8, 8, s.shape[1]).sum(axis=0)

    def weighted(v, p, lsum):
        return jnp.concatenate([jnp.dot(v, p, preferred_element_type=F32), lsum], axis=0)

    tk = ka_ref.shape[1]
    units = []
    for hd in range(A_HEADS):
        h = hd // A_GROUP
        q = qa_ref[0, hd * A_HEAD_DIM:(hd + 1) * A_HEAD_DIM, :]
        units.append((acc_a, hd, ka_ref, padded(q, h, A_KV_HEADS), h, va_ref))
    for u in range(2 * C_HEADS):
        q = qc_ref[0, u * C_QK_DIM:(u + 1) * C_QK_DIM, :]
        units.append((acc_c, u, kc_ref, padded(q, u, 2 * C_HEADS), u // 2, vc_ref))

    if online:
        for acc_ref, u, k_ref, qpad, vh, v_ref in units:
            m_ref = m_a if acc_ref is acc_a else m_c
            s = jnp.dot(k_ref[0], qpad, preferred_element_type=F32)
            m_prev = m_ref[u:u + 1, :]
            m_new = jnp.maximum(m_prev, jnp.max(s, axis=0, keepdims=True))
            m_ref[u:u + 1, :] = m_new
            acc_ref[u] = (acc_ref[u] * jnp.exp2(m_prev - m_new)
                          + weighted(v_ref[0, vh], *probs(s, m_new)))
    else:
        n_chunks = tk // KEY_CHUNK
        pending, partial = [], {}

        def retire():
            ui, ci, p, lsum = pending.pop(0)
            acc_ref, u, _, _, vh, v_ref = units[ui]
            pv = weighted(v_ref[0, vh, :, ci * KEY_CHUNK:(ci + 1) * KEY_CHUNK], p, lsum)
            partial[ui] = pv if ci == 0 else partial[ui] + pv
            if ci == n_chunks - 1:
                acc_ref[u] += partial.pop(ui)

        for ui, (acc_ref, u, k_ref, qpad, vh, v_ref) in enumerate(units):
            stab = stab_ref[0] if acc_ref is acc_a else stab_ref[1]
            for ci in range(n_chunks):
                s = jnp.dot(k_ref[0, ci * KEY_CHUNK:(ci + 1) * KEY_CHUNK, :], qpad,
                            preferred_element_type=F32)
                pending.append((ui, ci) + probs(s, stab))
                if len(pending) > PIPELINE_DEPTH:
                    retire()
        while pending:
            retire()

    @pl.when(kv == pl.num_programs(2) - 1)
    def _finish():
        def normalised(acc, d):
            l = jnp.sum(acc[d:], axis=0, keepdims=True)
            return acc[:d] / l, l

        lmin = None
        for hd in range(A_HEADS):
            o, l = normalised(acc_a[hd], A_HEAD_DIM)
            lmin = l if lmin is None else jnp.minimum(lmin, l)
            oa_ref[0, hd * A_HEAD_DIM:(hd + 1) * A_HEAD_DIM, :] = o.astype(BF16)
        lam_c = lam_ref[...]
        lam = (jnp.exp(jnp.sum(lam_c[0:1] * lam_c[1:2], axis=1, keepdims=True))
               - jnp.exp(jnp.sum(lam_c[2:3] * lam_c[3:4], axis=1, keepdims=True)) + lam_init)
        for h in range(C_HEADS):
            (o1, l1), (o2, l2) = (normalised(acc_c[2 * h + j], C_V_DIM) for j in range(2))
            lmin = jnp.minimum(lmin, jnp.minimum(l1, l2))
            o = o1 - lam * o2
            ms = jnp.mean(o * o, axis=0, keepdims=True)
            o = o * lax.rsqrt(ms + EPS) * gsub_ref[...] * (1.0 - lam_init)
            oc_ref[0, h * C_V_DIM:(h + 1) * C_V_DIM, :] = o.astype(BF16)
        if not online:
            lmin_ref[0, 0] = jnp.broadcast_to(lmin, lmin_ref.shape[2:])


def _attention(stab, lam_c, g_subln, qa, qc, ka, kc, va, vc, lam_init, online):
    b, _, t = qa.shape
    tkv = ka.shape[1]
    if online:
        tq, tk = _pick(t, (256, 128)), _pick(tkv, (1280, 1024, 512, 256))
    else:
        tq = _pick(t, (512, 256, 128))
        tk = _pick(tkv, (3328, 1280, 1024, 512, 256))
    nq = t // tq
    in_specs = [pl.BlockSpec((4, C_QK_DIM), lambda bi, i, j: (0, 0)),
                pl.BlockSpec((C_V_DIM, 1), lambda bi, i, j: (0, 0)),
                pl.BlockSpec((1, A_WIDTH, tq), lambda bi, i, j: (bi, 0, i)),
                pl.BlockSpec((1, C_QK_WIDTH, tq), lambda bi, i, j: (bi, 0, i)),
                pl.BlockSpec((1, tk, A_KV_WIDTH), lambda bi, i, j: (bi, j, 0)),
                pl.BlockSpec((1, tk, C_QK_WIDTH), lambda bi, i, j: (bi, j, 0)),
                pl.BlockSpec((1, A_KV_HEADS, A_HEAD_DIM, tk), lambda bi, i, j: (bi, 0, 0, j)),
                pl.BlockSpec((1, C_HEADS, C_V_DIM, tk), lambda bi, i, j: (bi, 0, 0, j))]
    out_specs = [pl.BlockSpec((1, A_WIDTH, tq), lambda bi, i, j: (bi, 0, i)),
                 pl.BlockSpec((1, C_WIDTH, tq), lambda bi, i, j: (bi, 0, i))]
    out_shape = [jax.ShapeDtypeStruct((b, A_WIDTH, t), BF16),
                 jax.ShapeDtypeStruct((b, C_WIDTH, t), BF16)]
    scratch = [pltpu.VMEM((A_HEADS, ACC_ROWS, tq), F32),
               pltpu.VMEM((2 * C_HEADS, ACC_ROWS, tq), F32)]
    args = (lam_c, g_subln, qa, qc, ka, kc, va, vc)
    if online:
        scratch += [pltpu.VMEM((A_HEADS, tq), F32), pltpu.VMEM((2 * C_HEADS, tq), F32)]
    else:
        in_specs = [pl.BlockSpec(memory_space=pltpu.SMEM)] + in_specs
        out_specs.append(pl.BlockSpec((1, 1, 8, tq), lambda bi, i, j: (bi, i, 0, 0)))
        out_shape.append(jax.ShapeDtypeStruct((b, nq, 8, tq), F32))
        args = (stab,) + args
    return pl.pallas_call(
        functools.partial(_attn_kernel, lam_init=lam_init, online=online),
        grid=(b, nq, tkv // tk),
        in_specs=in_specs, out_specs=out_specs, out_shape=out_shape, scratch_shapes=scratch,
        compiler_params=pltpu.CompilerParams(
            dimension_semantics=("parallel", "parallel", "arbitrary"),
            vmem_limit_bytes=VMEM_LIMIT),
        name="attention_online" if online else "attention_fixed",
    )(*args)


def _attend(gains_a, gains_c, lam_c, g_subln, qa, qc, ka, kc, va, vc, lam_init):
    bound = lambda gains, d: (1.02 * LOG2E * d ** 0.5
                              * jnp.max(jnp.abs(gains[0])) * jnp.max(jnp.abs(gains[1])))
    stab = jnp.stack([bound(gains_a, A_HEAD_DIM), bound(gains_c, C_QK_DIM)]) - STAB_MARGIN
    args = (lam_c, g_subln, qa, qc, ka, kc, va, vc, lam_init)
    oa, oc, lmin = _attention(stab.astype(F32), *args, online=False)
    return lax.cond(jnp.all(lmin >= MIN_DENOM), lambda: (oa, oc),
                    lambda: tuple(_attention(None, *args, online=True)))


def _merge_kernel(x_ref, gate_ref, oa_ref, oc_ref, rest_ref, gsgu_ref, wsp_ref, bsp_ref,
                  wpa_ref, wpb_ref, wpc_ref, wout_ref, o_ref, mt_ref):
    tm = x_ref.shape[1]
    rest = lambda r0, n: rest_ref[0, r0:r0 + n, :].astype(F32)

    ha = (oa_ref[0].astype(F32) * _silu(rest(S_ZA, A_WIDTH))).astype(BF16)
    ya = jnp.dot(wpa_ref[...], ha, preferred_element_type=F32)
    hc = (oc_ref[0].astype(F32) * _silu(rest(S_ZC, C_WIDTH))).astype(BF16)
    yc = jnp.dot(wpc_ref[...], hc, preferred_element_type=F32)

    mixed_groups = []
    for g in range(B_GROUPS):
        v = rest(S_BV + g * B_GROUP_DIM, B_GROUP_DIM)
        ms = jnp.mean(v * v, axis=0, keepdims=True)
        vn = (v * lax.rsqrt(ms + EPS) * gsgu_ref[g]).astype(BF16)
        cols = [jnp.dot(vn[:, c:c + SGU_BD], wsp_ref[g], preferred_element_type=F32) + bsp_ref[g]
                for c in range(0, tm, SGU_BD)]
        mixed_groups.append(cols[0] if len(cols) == 1 else jnp.concatenate(cols, axis=1))
    mixed = jnp.concatenate(mixed_groups, axis=0)
    hb = (rest(S_BU, B_WIDTH) * mixed * _silu(rest(S_BZ, B_WIDTH))).astype(BF16)
    yb = jnp.dot(wpb_ref[...], hb, preferred_element_type=F32)

    m = (_sigmoid(rest(S_GATES, D_MODEL)) * ya
         + _sigmoid(rest(S_GATES + D_MODEL, D_MODEL)) * yb
         + _sigmoid(rest(S_GATES + 2 * D_MODEL, D_MODEL)) * yc)
    mt_ref[...] = m.T.astype(BF16)
    out = jnp.dot(mt_ref[...], wout_ref[...], preferred_element_type=F32)
    o_ref[0] = x_ref[0] + gate_ref[0] * out


def _merge(x, gate, oa, oc, rest, gsgu, wsp_bd, bsp, wpa_t, wpb_t, wpc_t, wout):
    b, t, _ = x.shape
    tm = _pick(t, (512, 256))
    tokT = lambda n: pl.BlockSpec((1, n, tm), lambda bi, i: (bi, 0, i))
    full = lambda shape: pl.BlockSpec(shape, lambda bi, i: (0,) * len(shape))
    return pl.pallas_call(
        _merge_kernel,
        grid=(b, t // tm),
        in_specs=[pl.BlockSpec((1, tm, D_MODEL), lambda bi, i: (bi, i, 0)),
                  pl.BlockSpec((1, 1, D_MODEL), lambda bi, i: (bi, 0, 0)),
                  tokT(A_WIDTH), tokT(C_WIDTH), tokT(REST_WIDTH),
                  full((B_GROUPS, B_GROUP_DIM, 1)), full((B_GROUPS, SGU_BD, SGU_BD)),
                  full((B_GROUPS, 1, SGU_BD)),
                  full((D_MODEL, A_WIDTH)), full((D_MODEL, B_WIDTH)), full((D_MODEL, C_WIDTH)),
                  full((D_MODEL, D_MODEL))],
        out_specs=pl.BlockSpec((1, tm, D_MODEL), lambda bi, i: (bi, i, 0)),
        out_shape=jax.ShapeDtypeStruct((b, t, D_MODEL), F32),
        scratch_shapes=[pltpu.VMEM((tm, D_MODEL), BF16)],
        compiler_params=pltpu.CompilerParams(
            dimension_semantics=("parallel", "parallel"), vmem_limit_bytes=VMEM_LIMIT),
        name="merge_out",
    )(x, gate, oa, oc, rest, gsgu, wsp_bd, bsp, wpa_t, wpb_t, wpc_t, wout)


def _deinterleaved(w, n_heads):
    dim = w.shape[0] // n_heads
    w = w.reshape(n_heads, dim // 2, 2, *w.shape[1:])
    return jnp.swapaxes(w, 1, 2).reshape(n_heads * dim, *w.shape[3:])


def _projection_rows(w_in):
    off = np.cumsum([0, A_WIDTH, C_QK_WIDTH, A_KV_WIDTH, A_KV_WIDTH, C_QK_WIDTH, C_WIDTH])
    w = w_in.T
    qa, qc, ka, va, kc, vc = (w[off[i]:off[i + 1]] for i in range(6))
    return jnp.concatenate(
        [_deinterleaved(qa, A_HEADS), _deinterleaved(qc, 2 * C_HEADS),
         _deinterleaved(ka, A_KV_HEADS), _deinterleaved(kc, 2 * C_HEADS), va, vc, w[off[6]:]],
        axis=0).astype(BF16)


def _rope_tables(t, dim):
    quarter = dim // 4
    tok = jnp.arange(t)
    inv = jnp.power(ROPE_THETA, -jnp.arange(quarter, dtype=F32) / quarter)
    ang = jnp.concatenate([(tok // GRID_W)[:, None].astype(F32) * inv,
                           (tok % GRID_W)[:, None].astype(F32) * inv], axis=-1)
    return jnp.cos(ang).T, jnp.sin(ang).T


def kernel(x, c, ctx, c_ctx, w_mod, b_mod, g_norm, w_in, gq_a, gk_a, gq_c, gk_c, g_sgu, w_sp,
           b_sp, lam_c, g_subln, w_pa, w_pb, w_pc, w_out):
    depth = w_in.shape[0]
    b, t, _ = x.shape
    n_ctx = ctx.shape[1]

    cs = jnp.zeros((8, D_MODEL), F32).at[:b].set(c).at[b].set(c_ctx)
    mod = _modulation(cs, w_mod, b_mod)

    cosa, sina = _rope_tables(t, A_HEAD_DIM)
    cosc, sinc = _rope_tables(t, C_QK_DIM)
    one = lambda n: jnp.ones((n, n_ctx), F32)
    zero = lambda n: jnp.zeros((n, n_ctx), F32)
    eye2 = jnp.eye(SGU_BD // B_CHUNK, dtype=F32)

    for l in range(depth):
        lam_init = 0.8 - 0.6 * math.exp(-0.3 * l)
        rowv = lambda v: v.reshape(-1, 1, D_MODEL)
        shift, scale, gate = (rowv(mod[l, :b, i * D_MODEL:(i + 1) * D_MODEL]) for i in range(3))
        shift_c, scale_c, gate_c = (
            rowv(jnp.broadcast_to(mod[l, b, i * D_MODEL:(i + 1) * D_MODEL], (b, D_MODEL)))
            for i in range(3))
        colv = lambda g: _deinterleaved(g, 1).reshape(-1, 1)
        proj = functools.partial(
            _project, g_norm=g_norm[l].reshape(1, D_MODEL), w_t=_projection_rows(w_in[l]),
            gqa=colv(gq_a[l] * (A_HEAD_DIM ** -0.5 * LOG2E)), gka=colv(gk_a[l]),
            gqc=colv(gq_c[l] * (C_QK_DIM ** -0.5 * LOG2E)), gkc=colv(gk_c[l]))

        qa, qc, ka, kc, va, vc, rest = proj(
            x, shift, scale, cosa=cosa, sina=sina, cosc=cosc, sinc=sinc)
        qa_x, qc_x, ka_x, kc_x, va_x, vc_x, rest_x = proj(
            ctx, shift_c, scale_c, cosa=one(A_HEAD_DIM // 2), sina=zero(A_HEAD_DIM // 2),
            cosc=one(C_QK_DIM // 2), sinc=zero(C_QK_DIM // 2))

        attend = functools.partial(
            _attend, (gq_a[l], gk_a[l]), (gq_c[l], gk_c[l]), lam_c[l],
            g_subln[l].reshape(C_V_DIM, 1), lam_init=lam_init)
        heads = lambda vT, n: vT.reshape(b, n, -1, vT.shape[2])
        va_x, vc_x = heads(va_x, A_KV_HEADS), heads(vc_x, C_HEADS)
        oa, oc = attend(
            qa, qc,
            jnp.concatenate([ka_x, ka], axis=1), jnp.concatenate([kc_x, kc], axis=1),
            jnp.concatenate([va_x, heads(va, A_KV_HEADS)], axis=3),
            jnp.concatenate([vc_x, heads(vc, C_HEADS)], axis=3))

        wsp_bd = jnp.einsum("ij,gqp->giqjp", eye2, jnp.swapaxes(w_sp[l], 1, 2)).reshape(
            B_GROUPS, SGU_BD, SGU_BD).astype(BF16)
        bsp = jnp.tile(b_sp[l], (1, SGU_BD // B_CHUNK)).reshape(B_GROUPS, 1, SGU_BD)
        merge = functools.partial(
            _merge, gsgu=g_sgu[l].reshape(B_GROUPS, B_GROUP_DIM, 1), wsp_bd=wsp_bd, bsp=bsp,
            wpa_t=w_pa[l].T.astype(BF16), wpb_t=w_pb[l].T.astype(BF16),
            wpc_t=w_pc[l].T.astype(BF16), wout=w_out[l].astype(BF16))
        x_new = merge(x, gate, oa, oc, rest)
        if l < depth - 1:
            oa_x, oc_x = attend(qa_x, qc_x, ka_x, kc_x, va_x, vc_x)
            ctx = merge(ctx, gate_c, oa_x, oc_x, rest_x)
        x = x_new
    return x
```

```python
import functools
import math

import numpy as np
import jax
import jax.numpy as jnp
from jax import lax
from jax.experimental import pallas as pl
from jax.experimental.pallas import tpu as pltpu

F32 = jnp.float32
BF16 = jnp.bfloat16

D_MODEL = 1024
GRID_W = 64
ROPE_THETA = 10000.0
EPS = 1e-6
N_BRANCH = 3

A_HEAD_DIM = 64
A_WIDTH = D_MODEL // 2
A_HEADS = A_WIDTH // A_HEAD_DIM
A_KV_HEADS = A_HEADS // 4
A_GROUP = A_HEADS // A_KV_HEADS
A_KV_WIDTH = A_KV_HEADS * A_HEAD_DIM

B_WIDTH = D_MODEL // 4
B_GROUP_DIM = 64
B_GROUPS = B_WIDTH // B_GROUP_DIM
B_CHUNK = 128

C_WIDTH = D_MODEL // 4
C_QK_DIM = 32
C_V_DIM = 2 * C_QK_DIM
C_HEADS = C_WIDTH // C_V_DIM
C_QK_WIDTH = C_HEADS * 2 * C_QK_DIM

PROJ_WIDTH = 6144
R_QA, R_QC, R_KA, R_KC, R_VA, R_VC, R_REST = 0, 512, 768, 896, 1152, 1280, 1536
REST_WIDTH = PROJ_WIDTH - R_REST
S_ZA, S_ZC, S_BU, S_BV, S_BZ, S_GATES = 0, 512, 768, 1024, 1280, 1536

ACC_ROWS = 72
KEY_CHUNK = 256
PIPELINE_DEPTH = 2
SGU_BD = 2 * B_CHUNK
VMEM_LIMIT = 56 * 1024 * 1024
LOG2E = math.log2(math.e)

MIN_DENOM = 2.0 ** -80
STAB_MARGIN = 40.0


def _pick(n, candidates):
    for c in candidates:
        if n % c == 0:
            return c
    raise ValueError(f"no tile for {n} in {candidates}")


def _sigmoid(x):
    return 0.5 * jnp.tanh(0.5 * x) + 0.5


def _silu(x):
    return x * _sigmoid(x)


def _mod_kernel(c_ref, w_ref, b_ref, o_ref):
    s = _silu(c_ref[...])
    o_ref[0] = jnp.dot(s, w_ref[0], preferred_element_type=F32,
                       precision=lax.Precision.HIGHEST) + b_ref[0]


def _modulation(cs, w_mod, b_mod):
    depth = w_mod.shape[0]
    nj = 3 * D_MODEL // 1024
    return pl.pallas_call(
        _mod_kernel,
        grid=(depth, nj),
        in_specs=[pl.BlockSpec((8, D_MODEL), lambda l, j: (0, 0)),
                  pl.BlockSpec((1, D_MODEL, 1024), lambda l, j: (l, 0, j)),
                  pl.BlockSpec((1, 1, 1024), lambda l, j: (l, 0, j))],
        out_specs=pl.BlockSpec((1, 8, 1024), lambda l, j: (l, 0, j)),
        out_shape=jax.ShapeDtypeStruct((depth, 8, 3 * D_MODEL), F32),
        compiler_params=pltpu.CompilerParams(
            dimension_semantics=("parallel", "parallel"), vmem_limit_bytes=VMEM_LIMIT),
        name="adaln_mod",
    )(cs, w_mod, b_mod.reshape(depth, 1, 3 * D_MODEL))


def _norm_rope(p, g, cos, sin):
    half = p.shape[0] // 2
    ms = jnp.mean(p * p, axis=0, keepdims=True)
    y = p * lax.rsqrt(ms + EPS) * g
    x0, x1 = y[:half], y[half:]
    return jnp.concatenate([x0 * cos - x1 * sin, x0 * sin + x1 * cos], axis=0)


def _proj_kernel(x_ref, shift_ref, scale_ref, gn_ref, w_ref, gqa_ref, gka_ref, gqc_ref, gkc_ref,
                 cosa_ref, sina_ref, cosc_ref, sinc_ref,
                 qa_ref, qc_ref, ka_ref, kc_ref, va_ref, vc_ref, rest_ref, hn_ref):
    x = x_ref[0]
    ms = jnp.mean(x * x, axis=1, keepdims=True)
    y = x * lax.rsqrt(ms + EPS) * gn_ref[...]
    hn_ref[...] = (y * (1.0 + scale_ref[0]) + shift_ref[0]).T.astype(BF16)

    def proj(r0, r1):
        return jnp.dot(w_ref[r0:r1, :], hn_ref[...], preferred_element_type=F32)

    cosa, sina, cosc, sinc = cosa_ref[...], sina_ref[...], cosc_ref[...], sinc_ref[...]

    pqa = proj(R_QA, R_QC)
    for h in range(A_HEADS):
        r = h * A_HEAD_DIM
        qa_ref[0, r:r + A_HEAD_DIM, :] = _norm_rope(
            pqa[r:r + A_HEAD_DIM], gqa_ref[...], cosa, sina).astype(BF16)
    pqc = proj(R_QC, R_KA)
    for u in range(2 * C_HEADS):
        r = u * C_QK_DIM
        qc_ref[0, r:r + C_QK_DIM, :] = _norm_rope(
            pqc[r:r + C_QK_DIM], gqc_ref[...], cosc, sinc).astype(BF16)
    pka = proj(R_KA, R_KC)
    for h in range(A_KV_HEADS):
        ka_ref[0, h] = _norm_rope(pka[h * A_HEAD_DIM:(h + 1) * A_HEAD_DIM], gka_ref[...],
                                  cosa, sina).T.astype(BF16)
    pkc = proj(R_KC, R_VA)
    for u in range(2 * C_HEADS):
        kc_ref[0, u] = _norm_rope(pkc[u * C_QK_DIM:(u + 1) * C_QK_DIM], gkc_ref[...],
                                  cosc, sinc).T.astype(BF16)
    va_ref[0] = proj(R_VA, R_VC).astype(BF16)
    vc_ref[0] = proj(R_VC, R_REST).astype(BF16)
    for r in range(R_REST, PROJ_WIDTH, 512):
        rest_ref[0, r - R_REST:r - R_REST + 512, :] = proj(r, r + 512).astype(BF16)


def _project(x, shift, scale, g_norm, w_t, gqa, gka, gqc, gkc, cosa, sina, cosc, sinc):
    b, t, _ = x.shape
    tm = _pick(t, (512, 256, 128))
    col = lambda n: pl.BlockSpec((n, 1), lambda bi, i: (0, 0))
    tok = lambda n: pl.BlockSpec((n, tm), lambda bi, i: (0, i))
    outT = lambda n: pl.BlockSpec((1, n, tm), lambda bi, i: (bi, 0, i))
    return pl.pallas_call(
        _proj_kernel,
        grid=(b, t // tm),
        in_specs=[pl.BlockSpec((1, tm, D_MODEL), lambda bi, i: (bi, i, 0)),
                  pl.BlockSpec((1, 1, D_MODEL), lambda bi, i: (bi, 0, 0)),
                  pl.BlockSpec((1, 1, D_MODEL), lambda bi, i: (bi, 0, 0)),
                  pl.BlockSpec((1, D_MODEL), lambda bi, i: (0, 0)),
                  pl.BlockSpec((PROJ_WIDTH, D_MODEL), lambda bi, i: (0, 0),
                               pipeline_mode=pl.Buffered(1)),
                  col(A_HEAD_DIM), col(A_HEAD_DIM), col(C_QK_DIM), col(C_QK_DIM),
                  tok(A_HEAD_DIM // 2), tok(A_HEAD_DIM // 2), tok(C_QK_DIM // 2), tok(C_QK_DIM // 2)],
        out_specs=[outT(A_WIDTH), outT(C_QK_WIDTH),
                   pl.BlockSpec((1, A_KV_HEADS, tm, A_HEAD_DIM), lambda bi, i: (bi, 0, i, 0)),
                   pl.BlockSpec((1, 2 * C_HEADS, tm, C_QK_DIM), lambda bi, i: (bi, 0, i, 0)),
                   outT(A_KV_WIDTH), outT(C_WIDTH), outT(REST_WIDTH)],
        out_shape=[jax.ShapeDtypeStruct((b, A_WIDTH, t), BF16),
                   jax.ShapeDtypeStruct((b, C_QK_WIDTH, t), BF16),
                   jax.ShapeDtypeStruct((b, A_KV_HEADS, t, A_HEAD_DIM), BF16),
                   jax.ShapeDtypeStruct((b, 2 * C_HEADS, t, C_QK_DIM), BF16),
                   jax.ShapeDtypeStruct((b, A_KV_WIDTH, t), BF16),
                   jax.ShapeDtypeStruct((b, C_WIDTH, t), BF16),
                   jax.ShapeDtypeStruct((b, REST_WIDTH, t), BF16)],
        scratch_shapes=[pltpu.VMEM((D_MODEL, tm), BF16)],
        compiler_params=pltpu.CompilerParams(
            dimension_semantics=("parallel", "parallel"), vmem_limit_bytes=VMEM_LIMIT),
        name="in_proj",
    )(x, shift, scale, g_norm, w_t, gqa, gka, gqc, gkc, cosa, sina, cosc, sinc)


def _attn_kernel(*refs, lam_init, online):
    if online:
        (lam_ref, gsub_ref, qa_ref, qc_ref, ka_ref, kc_ref, va_ref, vc_ref,
         oa_ref, oc_ref, acc_a, acc_c, m_a, m_c) = refs
    else:
        (stab_ref, lam_ref, gsub_ref, qa_ref, qc_ref, ka_ref, kc_ref, va_ref, vc_ref,
         oa_ref, oc_ref, lmin_ref, acc_a, acc_c) = refs
        m_a = m_c = None
    kv = pl.program_id(2)

    @pl.when(kv == 0)
    def _init():
        acc_a[...] = jnp.zeros_like(acc_a)
        acc_c[...] = jnp.zeros_like(acc_c)
        if online:
            m_a[...] = jnp.full_like(m_a, -1e30)
            m_c[...] = jnp.full_like(m_c, -1e30)

    def probs(s, shift):
        pf = jnp.exp2(s - shift)
        return pf.astype(BF16), pf.reshape(s.shape[0] // 8, 8, s.shape[1]).sum(axis=0)

    def weighted(v, p, lsum):
        return jnp.concatenate([jnp.dot(v, p, preferred_element_type=F32), lsum], axis=0)

    tk = ka_ref.shape[2]
    units = []
    for hd in range(A_HEADS):
        q = qa_ref[0, hd * A_HEAD_DIM:(hd + 1) * A_HEAD_DIM, :]
        units.append((acc_a, hd, ka_ref, hd // A_GROUP, q, hd // A_GROUP, va_ref))
    for u in range(2 * C_HEADS):
        q = qc_ref[0, u * C_QK_DIM:(u + 1) * C_QK_DIM, :]
        units.append((acc_c, u, kc_ref, u, q, u // 2, vc_ref))

    if online:
        for acc_ref, u, k_ref, kh, q, vh, v_ref in units:
            m_ref = m_a if acc_ref is acc_a else m_c
            s = jnp.dot(k_ref[0, kh], q, preferred_element_type=F32)
            m_prev = m_ref[u:u + 1, :]
            m_new = jnp.maximum(m_prev, jnp.max(s, axis=0, keepdims=True))
            m_ref[u:u + 1, :] = m_new
            acc_ref[u] = (acc_ref[u] * jnp.exp2(m_prev - m_new)
                          + weighted(v_ref[0, vh], *probs(s, m_new)))
    else:
        n_chunks = tk // KEY_CHUNK
        pending, partial = [], {}

        def retire():
            ui, ci, p, lsum = pending.pop(0)
            acc_ref, u, _, _, _, vh, v_ref = units[ui]
            pv = weighted(v_ref[0, vh, :, ci * KEY_CHUNK:(ci + 1) * KEY_CHUNK], p, lsum)
            partial[ui] = pv if ci == 0 else partial[ui] + pv
            if ci == n_chunks - 1:
                acc_ref[u] += partial.pop(ui)

        for ui, (acc_ref, u, k_ref, kh, q, vh, v_ref) in enumerate(units):
            stab = stab_ref[0] if acc_ref is acc_a else stab_ref[1]
            for ci in range(n_chunks):
                s = jnp.dot(k_ref[0, kh, ci * KEY_CHUNK:(ci + 1) * KEY_CHUNK, :], q,
                            preferred_element_type=F32)
                pending.append((ui, ci) + probs(s, stab))
                if len(pending) > PIPELINE_DEPTH:
                    retire()
        while pending:
            retire()

    @pl.when(kv == pl.num_programs(2) - 1)
    def _finish():
        def normalised(acc, d):
            l = jnp.sum(acc[d:], axis=0, keepdims=True)
            return acc[:d] / l, l

        lmin = None
        for hd in range(A_HEADS):
            o, l = normalised(acc_a[hd], A_HEAD_DIM)
            lmin = l if lmin is None else jnp.minimum(lmin, l)
            oa_ref[0, hd * A_HEAD_DIM:(hd + 1) * A_HEAD_DIM, :] = o.astype(BF16)
        lam_c = lam_ref[...]
        lam = (jnp.exp(jnp.sum(lam_c[0:1] * lam_c[1:2], axis=1, keepdims=True))
               - jnp.exp(jnp.sum(lam_c[2:3] * lam_c[3:4], axis=1, keepdims=True)) + lam_init)
        for h in range(C_HEADS):
            (o1, l1), (o2, l2) = (normalised(acc_c[2 * h + j], C_V_DIM) for j in range(2))
            lmin = jnp.minimum(lmin, jnp.minimum(l1, l2))
            o = o1 - lam * o2
            ms = jnp.mean(o * o, axis=0, keepdims=True)
            o = o * lax.rsqrt(ms + EPS) * gsub_ref[...] * (1.0 - lam_init)
            oc_ref[0, h * C_V_DIM:(h + 1) * C_V_DIM, :] = o.astype(BF16)
        if not online:
            lmin_ref[0, 0] = jnp.broadcast_to(lmin, lmin_ref.shape[2:])


def _attention(stab, lam_c, g_subln, qa, qc, ka, kc, va, vc, lam_init, online):
    b, _, t = qa.shape
    tkv = ka.shape[2]
    if online:
        tq, tk = _pick(t, (256, 128)), _pick(tkv, (1280, 1024, 512, 256))
    else:
        tq = _pick(t, (512, 256, 128))
        tk = _pick(tkv, (3328, 1280, 1024, 512, 256))
    nq = t // tq
    in_specs = [pl.BlockSpec((4, C_QK_DIM), lambda bi, i, j: (0, 0)),
                pl.BlockSpec((C_V_DIM, 1), lambda bi, i, j: (0, 0)),
                pl.BlockSpec((1, A_WIDTH, tq), lambda bi, i, j: (bi, 0, i)),
                pl.BlockSpec((1, C_QK_WIDTH, tq), lambda bi, i, j: (bi, 0, i)),
                pl.BlockSpec((1, A_KV_HEADS, tk, A_HEAD_DIM), lambda bi, i, j: (bi, 0, j, 0)),
                pl.BlockSpec((1, 2 * C_HEADS, tk, C_QK_DIM), lambda bi, i, j: (bi, 0, j, 0)),
                pl.BlockSpec((1, A_KV_HEADS, A_HEAD_DIM, tk), lambda bi, i, j: (bi, 0, 0, j)),
                pl.BlockSpec((1, C_HEADS, C_V_DIM, tk), lambda bi, i, j: (bi, 0, 0, j))]
    out_specs = [pl.BlockSpec((1, A_WIDTH, tq), lambda bi, i, j: (bi, 0, i)),
                 pl.BlockSpec((1, C_WIDTH, tq), lambda bi, i, j: (bi, 0, i))]
    out_shape = [jax.ShapeDtypeStruct((b, A_WIDTH, t), BF16),
                 jax.ShapeDtypeStruct((b, C_WIDTH, t), BF16)]
    scratch = [pltpu.VMEM((A_HEADS, ACC_ROWS, tq), F32),
               pltpu.VMEM((2 * C_HEADS, ACC_ROWS, tq), F32)]
    args = (lam_c, g_subln, qa, qc, ka, kc, va, vc)
    if online:
        scratch += [pltpu.VMEM((A_HEADS, tq), F32), pltpu.VMEM((2 * C_HEADS, tq), F32)]
    else:
        in_specs = [pl.BlockSpec(memory_space=pltpu.SMEM)] + in_specs
        out_specs.append(pl.BlockSpec((1, 1, 8, tq), lambda bi, i, j: (bi, i, 0, 0)))
        out_shape.append(jax.ShapeDtypeStruct((b, nq, 8, tq), F32))
        args = (stab,) + args
    return pl.pallas_call(
        functools.partial(_attn_kernel, lam_init=lam_init, online=online),
        grid=(b, nq, tkv // tk),
        in_specs=in_specs, out_specs=out_specs, out_shape=out_shape, scratch_shapes=scratch,
        compiler_params=pltpu.CompilerParams(
            dimension_semantics=("parallel", "parallel", "arbitrary"),
            vmem_limit_bytes=VMEM_LIMIT),
        name="attention_online" if online else "attention_fixed",
    )(*args)


def _attend(gains_a, gains_c, lam_c, g_subln, qa, qc, ka, kc, va, vc, lam_init):
    bound = lambda gains, d: (1.02 * LOG2E * d ** 0.5
                              * jnp.max(jnp.abs(gains[0])) * jnp.max(jnp.abs(gains[1])))
    stab = jnp.stack([bound(gains_a, A_HEAD_DIM), bound(gains_c, C_QK_DIM)]) - STAB_MARGIN
    args = (lam_c, g_subln, qa, qc, ka, kc, va, vc, lam_init)
    oa, oc, lmin = _attention(stab.astype(F32), *args, online=False)
    return lax.cond(jnp.all(lmin >= MIN_DENOM), lambda: (oa, oc),
                    lambda: tuple(_attention(None, *args, online=True)))


def _merge_kernel(x_ref, gate_ref, oa_ref, oc_ref, rest_ref, gsgu_ref, wsp_ref, bsp_ref,
                  wpa_ref, wpb_ref, wpc_ref, wout_ref, o_ref, mt_ref):
    tm = x_ref.shape[1]
    rest = lambda r0, n: rest_ref[0, r0:r0 + n, :].astype(F32)

    ha = (oa_ref[0].astype(F32) * _silu(rest(S_ZA, A_WIDTH))).astype(BF16)
    ya = jnp.dot(wpa_ref[...], ha, preferred_element_type=F32)
    hc = (oc_ref[0].astype(F32) * _silu(rest(S_ZC, C_WIDTH))).astype(BF16)
    yc = jnp.dot(wpc_ref[...], hc, preferred_element_type=F32)

    mixed_groups = []
    for g in range(B_GROUPS):
        v = rest(S_BV + g * B_GROUP_DIM, B_GROUP_DIM)
        ms = jnp.mean(v * v, axis=0, keepdims=True)
        vn = (v * lax.rsqrt(ms + EPS) * gsgu_ref[g]).astype(BF16)
        cols = [jnp.dot(vn[:, c:c + SGU_BD], wsp_ref[g], preferred_element_type=F32) + bsp_ref[g]
                for c in range(0, tm, SGU_BD)]
        mixed_groups.append(cols[0] if len(cols) == 1 else jnp.concatenate(cols, axis=1))
    mixed = jnp.concatenate(mixed_groups, axis=0)
    hb = (rest(S_BU, B_WIDTH) * mixed * _silu(rest(S_BZ, B_WIDTH))).astype(BF16)
    yb = jnp.dot(wpb_ref[...], hb, preferred_element_type=F32)

    m = (_sigmoid(rest(S_GATES, D_MODEL)) * ya
         + _sigmoid(rest(S_GATES + D_MODEL, D_MODEL)) * yb
         + _sigmoid(rest(S_GATES + 2 * D_MODEL, D_MODEL)) * yc)
    mt_ref[...] = m.T.astype(BF16)
    out = jnp.dot(mt_ref[...], wout_ref[...], preferred_element_type=F32)
    o_ref[0] = x_ref[0] + gate_ref[0] * out


def _merge(x, gate, oa, oc, rest, gsgu, wsp_bd, bsp, wpa_t, wpb_t, wpc_t, wout):
    b, t, _ = x.shape
    tm = _pick(t, (512, 256))
    tokT = lambda n: pl.BlockSpec((1, n, tm), lambda bi, i: (bi, 0, i))
    full = lambda shape: pl.BlockSpec(shape, lambda bi, i: (0,) * len(shape))
    return pl.pallas_call(
        _merge_kernel,
        grid=(b, t // tm),
        in_specs=[pl.BlockSpec((1, tm, D_MODEL), lambda bi, i: (bi, i, 0)),
                  pl.BlockSpec((1, 1, D_MODEL), lambda bi, i: (bi, 0, 0)),
                  tokT(A_WIDTH), tokT(C_WIDTH), tokT(REST_WIDTH),
                  full((B_GROUPS, B_GROUP_DIM, 1)), full((B_GROUPS, SGU_BD, SGU_BD)),
                  full((B_GROUPS, 1, SGU_BD)),
                  full((D_MODEL, A_WIDTH)), full((D_MODEL, B_WIDTH)), full((D_MODEL, C_WIDTH)),
                  full((D_MODEL, D_MODEL))],
        out_specs=pl.BlockSpec((1, tm, D_MODEL), lambda bi, i: (bi, i, 0)),
        out_shape=jax.ShapeDtypeStruct((b, t, D_MODEL), F32),
        scratch_shapes=[pltpu.VMEM((tm, D_MODEL), BF16)],
        compiler_params=pltpu.CompilerParams(
            dimension_semantics=("parallel", "parallel"), vmem_limit_bytes=VMEM_LIMIT),
        name="merge_out",
    )(x, gate, oa, oc, rest, gsgu, wsp_bd, bsp, wpa_t, wpb_t, wpc_t, wout)


def _deinterleaved(w, n_heads):
    dim = w.shape[0] // n_heads
    w = w.reshape(n_heads, dim // 2, 2, *w.shape[1:])
    return jnp.swapaxes(w, 1, 2).reshape(n_heads * dim, *w.shape[3:])


def _projection_rows(w_in):
    off = np.cumsum([0, A_WIDTH, C_QK_WIDTH, A_KV_WIDTH, A_KV_WIDTH, C_QK_WIDTH, C_WIDTH])
    w = w_in.T
    qa, qc, ka, va, kc, vc = (w[off[i]:off[i + 1]] for i in range(6))
    return jnp.concatenate(
        [_deinterleaved(qa, A_HEADS), _deinterleaved(qc, 2 * C_HEADS),
         _deinterleaved(ka, A_KV_HEADS), _deinterleaved(kc, 2 * C_HEADS), va, vc, w[off[6]:]],
        axis=0).astype(BF16)


def _rope_tables(t, dim):
    quarter = dim // 4
    tok = jnp.arange(t)
    inv = jnp.power(ROPE_THETA, -jnp.arange(quarter, dtype=F32) / quarter)
    ang = jnp.concatenate([(tok // GRID_W)[:, None].astype(F32) * inv,
                           (tok % GRID_W)[:, None].astype(F32) * inv], axis=-1)
    return jnp.cos(ang).T, jnp.sin(ang).T


def kernel(x, c, ctx, c_ctx, w_mod, b_mod, g_norm, w_in, gq_a, gk_a, gq_c, gk_c, g_sgu, w_sp,
           b_sp, lam_c, g_subln, w_pa, w_pb, w_pc, w_out):
    depth = w_in.shape[0]
    b, t, _ = x.shape
    n_ctx = ctx.shape[1]

    cs = jnp.zeros((8, D_MODEL), F32).at[:b].set(c).at[b].set(c_ctx)
    mod = _modulation(cs, w_mod, b_mod)

    cosa, sina = _rope_tables(t, A_HEAD_DIM)
    cosc, sinc = _rope_tables(t, C_QK_DIM)
    one = lambda n: jnp.ones((n, n_ctx), F32)
    zero = lambda n: jnp.zeros((n, n_ctx), F32)
    eye2 = jnp.eye(SGU_BD // B_CHUNK, dtype=F32)

    for l in range(depth):
        lam_init = 0.8 - 0.6 * math.exp(-0.3 * l)
        rowv = lambda v: v.reshape(-1, 1, D_MODEL)
        shift, scale, gate = (rowv(mod[l, :b, i * D_MODEL:(i + 1) * D_MODEL]) for i in range(3))
        shift_c, scale_c, gate_c = (
            rowv(jnp.broadcast_to(mod[l, b, i * D_MODEL:(i + 1) * D_MODEL], (b, D_MODEL)))
            for i in range(3))
        colv = lambda g: _deinterleaved(g, 1).reshape(-1, 1)
        proj = functools.partial(
            _project, g_norm=g_norm[l].reshape(1, D_MODEL), w_t=_projection_rows(w_in[l]),
            gqa=colv(gq_a[l] * (A_HEAD_DIM ** -0.5 * LOG2E)), gka=colv(gk_a[l]),
            gqc=colv(gq_c[l] * (C_QK_DIM ** -0.5 * LOG2E)), gkc=colv(gk_c[l]))

        qa, qc, ka, kc, va, vc, rest = proj(
            x, shift, scale, cosa=cosa, sina=sina, cosc=cosc, sinc=sinc)
        qa_x, qc_x, ka_x, kc_x, va_x, vc_x, rest_x = proj(
            ctx, shift_c, scale_c, cosa=one(A_HEAD_DIM // 2), sina=zero(A_HEAD_DIM // 2),
            cosc=one(C_QK_DIM // 2), sinc=zero(C_QK_DIM // 2))

        attend = functools.partial(
            _attend, (gq_a[l], gk_a[l]), (gq_c[l], gk_c[l]), lam_c[l],
            g_subln[l].reshape(C_V_DIM, 1), lam_init=lam_init)
        heads = lambda vT, n: vT.reshape(b, n, -1, vT.shape[2])
        va_x, vc_x = heads(va_x, A_KV_HEADS), heads(vc_x, C_HEADS)
        oa, oc = attend(
            qa, qc,
            jnp.concatenate([ka_x, ka], axis=2), jnp.concatenate([kc_x, kc], axis=2),
            jnp.concatenate([va_x, heads(va, A_KV_HEADS)], axis=3),
            jnp.concatenate([vc_x, heads(vc, C_HEADS)], axis=3))

        wsp_bd = jnp.einsum("ij,gqp->giqjp", eye2, jnp.swapaxes(w_sp[l], 1, 2)).reshape(
            B_GROUPS, SGU_BD, SGU_BD).astype(BF16)
        bsp = jnp.tile(b_sp[l], (1, SGU_BD // B_CHUNK)).reshape(B_GROUPS, 1, SGU_BD)
        merge = functools.partial(
            _merge, gsgu=g_sgu[l].reshape(B_GROUPS, B_GROUP_DIM, 1), wsp_bd=wsp_bd, bsp=bsp,
            wpa_t=w_pa[l].T.astype(BF16), wpb_t=w_pb[l].T.astype(BF16),
            wpc_t=w_pc[l].T.astype(BF16), wout=w_out[l].astype(BF16))
        x_new = merge(x, gate, oa, oc, rest)
        if l < depth - 1:
            oa_x, oc_x = attend(qa_x, qc_x, ka_x, kc_x, va_x, vc_x)
            ctx = merge(ctx, gate_c, oa_x, oc_x, rest_x)
        x = x_new
    return x
```

```python
import functools
import math

import numpy as np
import jax
import jax.numpy as jnp
from jax import lax
from jax.experimental import pallas as pl
from jax.experimental.pallas import tpu as pltpu

F32 = jnp.float32
BF16 = jnp.bfloat16

D_MODEL = 1024
GRID_W = 64
ROPE_THETA = 10000.0
EPS = 1e-6
N_BRANCH = 3

A_HEAD_DIM = 64
A_WIDTH = D_MODEL // 2
A_HEADS = A_WIDTH // A_HEAD_DIM
A_KV_HEADS = A_HEADS // 4
A_GROUP = A_HEADS // A_KV_HEADS
A_KV_WIDTH = A_KV_HEADS * A_HEAD_DIM

B_WIDTH = D_MODEL // 4
B_GROUP_DIM = 64
B_GROUPS = B_WIDTH // B_GROUP_DIM
B_CHUNK = 128

C_WIDTH = D_MODEL // 4
C_QK_DIM = 32
C_V_DIM = 2 * C_QK_DIM
C_HEADS = C_WIDTH // C_V_DIM
C_QK_WIDTH = C_HEADS * 2 * C_QK_DIM

PROJ_WIDTH = 6144
R_QA, R_QC, R_KA, R_KC, R_VA, R_VC, R_REST = 0, 512, 768, 896, 1152, 1280, 1536
REST_WIDTH = PROJ_WIDTH - R_REST
S_ZA, S_ZC, S_BU, S_BV, S_BZ, S_GATES = 0, 512, 768, 1024, 1280, 1536

ACC_ROWS = 72
KEY_CHUNK = 256
PIPELINE_DEPTH = 2
SGU_BD = 2 * B_CHUNK
VMEM_LIMIT = 56 * 1024 * 1024
LOG2E = math.log2(math.e)

MIN_DENOM = 2.0 ** -80
STAB_MARGIN = 40.0


def _pick(n, candidates):
    for c in candidates:
        if n % c == 0:
            return c
    raise ValueError(f"no tile for {n} in {candidates}")


def _sigmoid(x):
    return 0.5 * jnp.tanh(0.5 * x) + 0.5


def _silu(x):
    return x * _sigmoid(x)


def _mod_kernel(c_ref, w_ref, b_ref, o_ref):
    s = _silu(c_ref[...])
    o_ref[0] = jnp.dot(s, w_ref[0], preferred_element_type=F32,
                       precision=lax.Precision.HIGHEST) + b_ref[0]


def _modulation(cs, w_mod, b_mod):
    depth = w_mod.shape[0]
    nj = 3 * D_MODEL // 1024
    return pl.pallas_call(
        _mod_kernel,
        grid=(depth, nj),
        in_specs=[pl.BlockSpec((8, D_MODEL), lambda l, j: (0, 0)),
                  pl.BlockSpec((1, D_MODEL, 1024), lambda l, j: (l, 0, j)),
                  pl.BlockSpec((1, 1, 1024), lambda l, j: (l, 0, j))],
        out_specs=pl.BlockSpec((1, 8, 1024), lambda l, j: (l, 0, j)),
        out_shape=jax.ShapeDtypeStruct((depth, 8, 3 * D_MODEL), F32),
        compiler_params=pltpu.CompilerParams(
            dimension_semantics=("parallel", "parallel"), vmem_limit_bytes=VMEM_LIMIT),
        name="adaln_mod",
    )(cs, w_mod, b_mod.reshape(depth, 1, 3 * D_MODEL))


def _norm_rope(p, g, cos, sin):
    half = p.shape[0] // 2
    ms = jnp.mean(p * p, axis=0, keepdims=True)
    y = p * lax.rsqrt(ms + EPS) * g
    x0, x1 = y[:half], y[half:]
    return jnp.concatenate([x0 * cos - x1 * sin, x0 * sin + x1 * cos], axis=0)


def _proj_kernel(*refs, n_tiles):
    i = pl.program_id(1)

    @pl.when(i < n_tiles)
    def _tile():
        _proj_tile(*refs)

    @pl.when(i >= n_tiles)
    def _tail():
        for kv_ref in refs[-6:-2]:
            kv_ref[...] = jnp.zeros_like(kv_ref)


def _proj_tile(x_ref, shift_ref, scale_ref, gn_ref, w_ref, gqa_ref, gka_ref, gqc_ref, gkc_ref,
               cosa_ref, sina_ref, cosc_ref, sinc_ref, *refs):
    qa_ref, qc_ref, ka_ref, kc_ref, va_ref, vc_ref, rest_ref, hn_ref = refs[-8:]
    x = x_ref[0]
    ms = jnp.mean(x * x, axis=1, keepdims=True)
    y = x * lax.rsqrt(ms + EPS) * gn_ref[...]
    hn_ref[...] = (y * (1.0 + scale_ref[0]) + shift_ref[0]).T.astype(BF16)

    def proj(r0, r1):
        return jnp.dot(w_ref[r0:r1, :], hn_ref[...], preferred_element_type=F32)

    cosa, sina, cosc, sinc = cosa_ref[...], sina_ref[...], cosc_ref[...], sinc_ref[...]

    pqa = proj(R_QA, R_QC)
    for h in range(A_HEADS):
        r = h * A_HEAD_DIM
        qa_ref[0, r:r + A_HEAD_DIM, :] = _norm_rope(
            pqa[r:r + A_HEAD_DIM], gqa_ref[...], cosa, sina).astype(BF16)
    pqc = proj(R_QC, R_KA)
    for u in range(2 * C_HEADS):
        r = u * C_QK_DIM
        qc_ref[0, r:r + C_QK_DIM, :] = _norm_rope(
            pqc[r:r + C_QK_DIM], gqc_ref[...], cosc, sinc).astype(BF16)
    pka = proj(R_KA, R_KC)
    for h in range(A_KV_HEADS):
        ka_ref[0, h] = _norm_rope(pka[h * A_HEAD_DIM:(h + 1) * A_HEAD_DIM], gka_ref[...],
                                  cosa, sina).T.astype(BF16)
    pkc = proj(R_KC, R_VA)
    for u in range(2 * C_HEADS):
        kc_ref[0, u] = _norm_rope(pkc[u * C_QK_DIM:(u + 1) * C_QK_DIM], gkc_ref[...],
                                  cosc, sinc).T.astype(BF16)
    va_ref[0] = proj(R_VA, R_VC).astype(BF16)
    vc_ref[0] = proj(R_VC, R_REST).astype(BF16)
    for r in range(R_REST, PROJ_WIDTH, 512):
        rest_ref[0, r - R_REST:r - R_REST + 512, :] = proj(r, r + 512).astype(BF16)


def _project(x, shift, scale, g_norm, w_t, gqa, gka, gqc, gkc, cosa, sina, cosc, sinc,
             kv_len, kv_start=0, kv_into=()):
    b, t, _ = x.shape
    tm = _pick(t, (512, 256, 128))
    assert kv_start % tm == 0
    blk0, n_tiles = kv_start // tm, t // tm
    n_tail = pl.cdiv(kv_len - kv_start - t, tm)
    tile = lambda i: jnp.minimum(i, n_tiles - 1)
    col = lambda n: pl.BlockSpec((n, 1), lambda bi, i: (0, 0))
    tok = lambda n: pl.BlockSpec((n, tm), lambda bi, i: (0, tile(i)))
    outT = lambda n: pl.BlockSpec((1, n, tm), lambda bi, i: (bi, 0, tile(i)))
    kv_tok = lambda n, d: pl.BlockSpec((1, n, tm, d), lambda bi, i: (bi, 0, i + blk0, 0))
    kv_chan = lambda n: pl.BlockSpec((1, n, tm), lambda bi, i: (bi, 0, i + blk0))
    n_in = 13
    return pl.pallas_call(
        functools.partial(_proj_kernel, n_tiles=n_tiles),
        grid=(b, n_tiles + n_tail),
        in_specs=[pl.BlockSpec((1, tm, D_MODEL), lambda bi, i: (bi, tile(i), 0)),
                  pl.BlockSpec((1, 1, D_MODEL), lambda bi, i: (bi, 0, 0)),
                  pl.BlockSpec((1, 1, D_MODEL), lambda bi, i: (bi, 0, 0)),
                  pl.BlockSpec((1, D_MODEL), lambda bi, i: (0, 0)),
                  pl.BlockSpec((PROJ_WIDTH, D_MODEL), lambda bi, i: (0, 0),
                               pipeline_mode=pl.Buffered(1)),
                  col(A_HEAD_DIM), col(A_HEAD_DIM), col(C_QK_DIM), col(C_QK_DIM),
                  tok(A_HEAD_DIM // 2), tok(A_HEAD_DIM // 2), tok(C_QK_DIM // 2), tok(C_QK_DIM // 2)]
                 + [pl.BlockSpec(memory_space=pl.ANY)] * len(kv_into),
        out_specs=[outT(A_WIDTH), outT(C_QK_WIDTH),
                   kv_tok(A_KV_HEADS, A_HEAD_DIM), kv_tok(2 * C_HEADS, C_QK_DIM),
                   kv_chan(A_KV_WIDTH), kv_chan(C_WIDTH), outT(REST_WIDTH)],
        out_shape=[jax.ShapeDtypeStruct((b, A_WIDTH, t), BF16),
                   jax.ShapeDtypeStruct((b, C_QK_WIDTH, t), BF16),
                   jax.ShapeDtypeStruct((b, A_KV_HEADS, kv_len, A_HEAD_DIM), BF16),
                   jax.ShapeDtypeStruct((b, 2 * C_HEADS, kv_len, C_QK_DIM), BF16),
                   jax.ShapeDtypeStruct((b, A_KV_WIDTH, kv_len), BF16),
                   jax.ShapeDtypeStruct((b, C_WIDTH, kv_len), BF16),
                   jax.ShapeDtypeStruct((b, REST_WIDTH, t), BF16)],
        input_output_aliases={n_in + j: 2 + j for j in range(len(kv_into))},
        scratch_shapes=[pltpu.VMEM((D_MODEL, tm), BF16)],
        compiler_params=pltpu.CompilerParams(
            dimension_semantics=("parallel", "arbitrary"), vmem_limit_bytes=VMEM_LIMIT),
        name="in_proj",
    )(x, shift, scale, g_norm, w_t, gqa, gka, gqc, gkc, cosa, sina, cosc, sinc, *kv_into)


def _attn_kernel(*refs, lam_init, online):
    if online:
        (lam_ref, gsub_ref, qa_ref, qc_ref, ka_ref, kc_ref, va_ref, vc_ref,
         oa_ref, oc_ref, acc_a, acc_c, m_a, m_c) = refs
    else:
        (stab_ref, lam_ref, gsub_ref, qa_ref, qc_ref, ka_ref, kc_ref, va_ref, vc_ref,
         oa_ref, oc_ref, lmin_ref, acc_a, acc_c) = refs
        m_a = m_c = None
    kv = pl.program_id(2)

    @pl.when(kv == 0)
    def _init():
        acc_a[...] = jnp.zeros_like(acc_a)
        acc_c[...] = jnp.zeros_like(acc_c)
        if online:
            m_a[...] = jnp.full_like(m_a, -1e30)
            m_c[...] = jnp.full_like(m_c, -1e30)

    def probs(s, shift):
        pf = jnp.exp2(s - shift)
        return pf.astype(BF16), pf.reshape(s.shape[0] // 8, 8, s.shape[1]).sum(axis=0)

    def weighted(v, p, lsum):
        return jnp.concatenate([jnp.dot(v, p, preferred_element_type=F32), lsum], axis=0)

    tk = ka_ref.shape[2]
    units = []
    for hd in range(A_HEADS):
        q = qa_ref[0, hd * A_HEAD_DIM:(hd + 1) * A_HEAD_DIM, :]
        units.append((acc_a, hd, ka_ref, hd // A_GROUP, q, hd // A_GROUP, va_ref))
    for u in range(2 * C_HEADS):
        q = qc_ref[0, u * C_QK_DIM:(u + 1) * C_QK_DIM, :]
        units.append((acc_c, u, kc_ref, u, q, u // 2, vc_ref))

    if online:
        for acc_ref, u, k_ref, kh, q, vh, v_ref in units:
            m_ref = m_a if acc_ref is acc_a else m_c
            s = jnp.dot(k_ref[0, kh], q, preferred_element_type=F32)
            m_prev = m_ref[u:u + 1, :]
            m_new = jnp.maximum(m_prev, jnp.max(s, axis=0, keepdims=True))
            m_ref[u:u + 1, :] = m_new
            acc_ref[u] = (acc_ref[u] * jnp.exp2(m_prev - m_new)
                          + weighted(v_ref[0, vh], *probs(s, m_new)))
    else:
        n_chunks = tk // KEY_CHUNK
        pending, partial = [], {}

        def retire():
            ui, ci, p, lsum = pending.pop(0)
            acc_ref, u, _, _, _, vh, v_ref = units[ui]
            pv = weighted(v_ref[0, vh, :, ci * KEY_CHUNK:(ci + 1) * KEY_CHUNK], p, lsum)
            partial[ui] = pv if ci == 0 else partial[ui] + pv
            if ci == n_chunks - 1:
                acc_ref[u] += partial.pop(ui)

        for ui, (acc_ref, u, k_ref, kh, q, vh, v_ref) in enumerate(units):
            stab = stab_ref[0] if acc_ref is acc_a else stab_ref[1]
            for ci in range(n_chunks):
                s = jnp.dot(k_ref[0, kh, ci * KEY_CHUNK:(ci + 1) * KEY_CHUNK, :], q,
                            preferred_element_type=F32)
                pending.append((ui, ci) + probs(s, stab))
                if len(pending) > PIPELINE_DEPTH:
                    retire()
        while pending:
            retire()

    @pl.when(kv == pl.num_programs(2) - 1)
    def _finish():
        def normalised(acc, d):
            l = jnp.sum(acc[d:], axis=0, keepdims=True)
            return acc[:d] / l, l

        lmin = None
        for hd in range(A_HEADS):
            o, l = normalised(acc_a[hd], A_HEAD_DIM)
            lmin = l if lmin is None else jnp.minimum(lmin, l)
            oa_ref[0, hd * A_HEAD_DIM:(hd + 1) * A_HEAD_DIM, :] = o.astype(BF16)
        lam_c = lam_ref[...]
        lam = (jnp.exp(jnp.sum(lam_c[0:1] * lam_c[1:2], axis=1, keepdims=True))
               - jnp.exp(jnp.sum(lam_c[2:3] * lam_c[3:4], axis=1, keepdims=True)) + lam_init)
        for h in range(C_HEADS):
            (o1, l1), (o2, l2) = (normalised(acc_c[2 * h + j], C_V_DIM) for j in range(2))
            lmin = jnp.minimum(lmin, jnp.minimum(l1, l2))
            o = o1 - lam * o2
            ms = jnp.mean(o * o, axis=0, keepdims=True)
            o = o * lax.rsqrt(ms + EPS) * gsub_ref[...] * (1.0 - lam_init)
            oc_ref[0, h * C_V_DIM:(h + 1) * C_V_DIM, :] = o.astype(BF16)
        if not online:
            lmin_ref[0, 0] = jnp.broadcast_to(lmin, lmin_ref.shape[2:])


def _attention(stab, lam_c, g_subln, qa, qc, ka, kc, va, vc, lam_init, online):
    b, _, t = qa.shape
    tkv = ka.shape[2]
    if online:
        tq, tk = _pick(t, (256, 128)), _pick(tkv, (1280, 1024, 512, 256))
    else:
        tq = _pick(t, (512, 256, 128))
        tk = _pick(tkv, (3328, 1280, 1024, 512, 256))
    nq = t // tq
    in_specs = [pl.BlockSpec((4, C_QK_DIM), lambda bi, i, j: (0, 0)),
                pl.BlockSpec((C_V_DIM, 1), lambda bi, i, j: (0, 0)),
                pl.BlockSpec((1, A_WIDTH, tq), lambda bi, i, j: (bi, 0, i)),
                pl.BlockSpec((1, C_QK_WIDTH, tq), lambda bi, i, j: (bi, 0, i)),
                pl.BlockSpec((1, A_KV_HEADS, tk, A_HEAD_DIM), lambda bi, i, j: (bi, 0, j, 0)),
                pl.BlockSpec((1, 2 * C_HEADS, tk, C_QK_DIM), lambda bi, i, j: (bi, 0, j, 0)),
                pl.BlockSpec((1, A_KV_HEADS, A_HEAD_DIM, tk), lambda bi, i, j: (bi, 0, 0, j)),
                pl.BlockSpec((1, C_HEADS, C_V_DIM, tk), lambda bi, i, j: (bi, 0, 0, j))]
    out_specs = [pl.BlockSpec((1, A_WIDTH, tq), lambda bi, i, j: (bi, 0, i)),
                 pl.BlockSpec((1, C_WIDTH, tq), lambda bi, i, j: (bi, 0, i))]
    out_shape = [jax.ShapeDtypeStruct((b, A_WIDTH, t), BF16),
                 jax.ShapeDtypeStruct((b, C_WIDTH, t), BF16)]
    scratch = [pltpu.VMEM((A_HEADS, ACC_ROWS, tq), F32),
               pltpu.VMEM((2 * C_HEADS, ACC_ROWS, tq), F32)]
    args = (lam_c, g_subln, qa, qc, ka, kc, va, vc)
    if online:
        scratch += [pltpu.VMEM((A_HEADS, tq), F32), pltpu.VMEM((2 * C_HEADS, tq), F32)]
    else:
        in_specs = [pl.BlockSpec(memory_space=pltpu.SMEM)] + in_specs
        out_specs.append(pl.BlockSpec((1, 1, 8, tq), lambda bi, i, j: (bi, i, 0, 0)))
        out_shape.append(jax.ShapeDtypeStruct((b, nq, 8, tq), F32))
        args = (stab,) + args
    return pl.pallas_call(
        functools.partial(_attn_kernel, lam_init=lam_init, online=online),
        grid=(b, nq, tkv // tk),
        in_specs=in_specs, out_specs=out_specs, out_shape=out_shape, scratch_shapes=scratch,
        compiler_params=pltpu.CompilerParams(
            dimension_semantics=("parallel", "parallel", "arbitrary"),
            vmem_limit_bytes=VMEM_LIMIT),
        name="attention_online" if online else "attention_fixed",
    )(*args)


def _attend(gains_a, gains_c, lam_c, g_subln, qa, qc, ka, kc, va, vc, lam_init):
    bound = lambda gains, d: (1.02 * LOG2E * d ** 0.5
                              * jnp.max(jnp.abs(gains[0])) * jnp.max(jnp.abs(gains[1])))
    stab = jnp.stack([bound(gains_a, A_HEAD_DIM), bound(gains_c, C_QK_DIM)]) - STAB_MARGIN
    args = (lam_c, g_subln, qa, qc, ka, kc, va, vc, lam_init)
    oa, oc, lmin = _attention(stab.astype(F32), *args, online=False)
    return lax.cond(jnp.all(lmin >= MIN_DENOM), lambda: (oa, oc),
                    lambda: tuple(_attention(None, *args, online=True)))


def _merge_kernel(x_ref, gate_ref, oa_ref, oc_ref, rest_ref, gsgu_ref, wsp_ref, bsp_ref,
                  wpa_ref, wpb_ref, wpc_ref, wout_ref, o_ref, mt_ref):
    tm = x_ref.shape[1]
    rest = lambda r0, n: rest_ref[0, r0:r0 + n, :].astype(F32)

    ha = (oa_ref[0].astype(F32) * _silu(rest(S_ZA, A_WIDTH))).astype(BF16)
    ya = jnp.dot(wpa_ref[...], ha, preferred_element_type=F32)
    hc = (oc_ref[0].astype(F32) * _silu(rest(S_ZC, C_WIDTH))).astype(BF16)
    yc = jnp.dot(wpc_ref[...], hc, preferred_element_type=F32)

    mixed_groups = []
    for g in range(B_GROUPS):
        v = rest(S_BV + g * B_GROUP_DIM, B_GROUP_DIM)
        ms = jnp.mean(v * v, axis=0, keepdims=True)
        vn = (v * lax.rsqrt(ms + EPS) * gsgu_ref[g]).astype(BF16)
        cols = [jnp.dot(vn[:, c:c + SGU_BD], wsp_ref[g], preferred_element_type=F32) + bsp_ref[g]
                for c in range(0, tm, SGU_BD)]
        mixed_groups.append(cols[0] if len(cols) == 1 else jnp.concatenate(cols, axis=1))
    mixed = jnp.concatenate(mixed_groups, axis=0)
    hb = (rest(S_BU, B_WIDTH) * mixed * _silu(rest(S_BZ, B_WIDTH))).astype(BF16)
    yb = jnp.dot(wpb_ref[...], hb, preferred_element_type=F32)

    m = (_sigmoid(rest(S_GATES, D_MODEL)) * ya
         + _sigmoid(rest(S_GATES + D_MODEL, D_MODEL)) * yb
         + _sigmoid(rest(S_GATES + 2 * D_MODEL, D_MODEL)) * yc)
    mt_ref[...] = m.T.astype(BF16)
    out = jnp.dot(mt_ref[...], wout_ref[...], preferred_element_type=F32)
    o_ref[0] = x_ref[0] + gate_ref[0] * out


def _merge(x, gate, oa, oc, rest, gsgu, wsp_bd, bsp, wpa_t, wpb_t, wpc_t, wout):
    b, t, _ = x.shape
    tm = _pick(t, (512, 256))
    tokT = lambda n: pl.BlockSpec((1, n, tm), lambda bi, i: (bi, 0, i))
    full = lambda shape: pl.BlockSpec(shape, lambda bi, i: (0,) * len(shape))
    return pl.pallas_call(
        _merge_kernel,
        grid=(b, t // tm),
        in_specs=[pl.BlockSpec((1, tm, D_MODEL), lambda bi, i: (bi, i, 0)),
                  pl.BlockSpec((1, 1, D_MODEL), lambda bi, i: (bi, 0, 0)),
                  tokT(A_WIDTH), tokT(C_WIDTH), tokT(REST_WIDTH),
                  full((B_GROUPS, B_GROUP_DIM, 1)), full((B_GROUPS, SGU_BD, SGU_BD)),
                  full((B_GROUPS, 1, SGU_BD)),
                  full((D_MODEL, A_WIDTH)), full((D_MODEL, B_WIDTH)), full((D_MODEL, C_WIDTH)),
                  full((D_MODEL, D_MODEL))],
        out_specs=pl.BlockSpec((1, tm, D_MODEL), lambda bi, i: (bi, i, 0)),
        out_shape=jax.ShapeDtypeStruct((b, t, D_MODEL), F32),
        scratch_shapes=[pltpu.VMEM((tm, D_MODEL), BF16)],
        compiler_params=pltpu.CompilerParams(
            dimension_semantics=("parallel", "parallel"), vmem_limit_bytes=VMEM_LIMIT),
        name="merge_out",
    )(x, gate, oa, oc, rest, gsgu, wsp_bd, bsp, wpa_t, wpb_t, wpc_t, wout)


def _deinterleaved(w, n_heads):
    dim = w.shape[0] // n_heads
    w = w.reshape(n_heads, dim // 2, 2, *w.shape[1:])
    return jnp.swapaxes(w, 1, 2).reshape(n_heads * dim, *w.shape[3:])


def _projection_rows(w_in):
    off = np.cumsum([0, A_WIDTH, C_QK_WIDTH, A_KV_WIDTH, A_KV_WIDTH, C_QK_WIDTH, C_WIDTH])
    w = w_in.T
    qa, qc, ka, va, kc, vc = (w[off[i]:off[i + 1]] for i in range(6))
    return jnp.concatenate(
        [_deinterleaved(qa, A_HEADS), _deinterleaved(qc, 2 * C_HEADS),
         _deinterleaved(ka, A_KV_HEADS), _deinterleaved(kc, 2 * C_HEADS), va, vc, w[off[6]:]],
        axis=0).astype(BF16)


def _rope_tables(t, dim):
    quarter = dim // 4
    tok = jnp.arange(t)
    inv = jnp.power(ROPE_THETA, -jnp.arange(quarter, dtype=F32) / quarter)
    ang = jnp.concatenate([(tok // GRID_W)[:, None].astype(F32) * inv,
                           (tok % GRID_W)[:, None].astype(F32) * inv], axis=-1)
    return jnp.cos(ang).T, jnp.sin(ang).T


def kernel(x, c, ctx, c_ctx, w_mod, b_mod, g_norm, w_in, gq_a, gk_a, gq_c, gk_c, g_sgu, w_sp,
           b_sp, lam_c, g_subln, w_pa, w_pb, w_pc, w_out):
    depth = w_in.shape[0]
    b, t, _ = x.shape
    n_ctx = ctx.shape[1]

    cs = jnp.zeros((8, D_MODEL), F32).at[:b].set(c).at[b].set(c_ctx)
    mod = _modulation(cs, w_mod, b_mod)

    cosa, sina = _rope_tables(t, A_HEAD_DIM)
    cosc, sinc = _rope_tables(t, C_QK_DIM)
    one = lambda n: jnp.ones((n, n_ctx), F32)
    zero = lambda n: jnp.zeros((n, n_ctx), F32)
    eye2 = jnp.eye(SGU_BD // B_CHUNK, dtype=F32)

    for l in range(depth):
        lam_init = 0.8 - 0.6 * math.exp(-0.3 * l)
        rowv = lambda v: v.reshape(-1, 1, D_MODEL)
        shift, scale, gate = (rowv(mod[l, :b, i * D_MODEL:(i + 1) * D_MODEL]) for i in range(3))
        shift_c, scale_c, gate_c = (
            rowv(jnp.broadcast_to(mod[l, b, i * D_MODEL:(i + 1) * D_MODEL], (b, D_MODEL)))
            for i in range(3))
        colv = lambda g: _deinterleaved(g, 1).reshape(-1, 1)
        proj = functools.partial(
            _project, g_norm=g_norm[l].reshape(1, D_MODEL), w_t=_projection_rows(w_in[l]),
            gqa=colv(gq_a[l] * (A_HEAD_DIM ** -0.5 * LOG2E)), gka=colv(gk_a[l]),
            gqc=colv(gq_c[l] * (C_QK_DIM ** -0.5 * LOG2E)), gkc=colv(gk_c[l]))

        qa, qc, ka, kc, va, vc, rest = proj(
            x, shift, scale, cosa=cosa, sina=sina, cosc=cosc, sinc=sinc, kv_len=t + n_ctx)
        qa_x, qc_x, ka, kc, va, vc, rest_x = proj(
            ctx, shift_c, scale_c, cosa=one(A_HEAD_DIM // 2), sina=zero(A_HEAD_DIM // 2),
            cosc=one(C_QK_DIM // 2), sinc=zero(C_QK_DIM // 2),
            kv_len=t + n_ctx, kv_start=t, kv_into=(ka, kc, va, vc))
        va, vc = va.reshape(b, A_KV_HEADS, -1, t + n_ctx), vc.reshape(b, C_HEADS, -1, t + n_ctx)

        attend = functools.partial(
            _attend, (gq_a[l], gk_a[l]), (gq_c[l], gk_c[l]), lam_c[l],
            g_subln[l].reshape(C_V_DIM, 1), lam_init=lam_init)
        oa, oc = attend(qa, qc, ka, kc, va, vc)

        wsp_bd = jnp.einsum("ij,gqp->giqjp", eye2, jnp.swapaxes(w_sp[l], 1, 2)).reshape(
            B_GROUPS, SGU_BD, SGU_BD).astype(BF16)
        bsp = jnp.tile(b_sp[l], (1, SGU_BD // B_CHUNK)).reshape(B_GROUPS, 1, SGU_BD)
        merge = functools.partial(
            _merge, gsgu=g_sgu[l].reshape(B_GROUPS, B_GROUP_DIM, 1), wsp_bd=wsp_bd, bsp=bsp,
            wpa_t=w_pa[l].T.astype(BF16), wpb_t=w_pb[l].T.astype(BF16),
            wpc_t=w_pc[l].T.astype(BF16), wout=w_out[l].astype(BF16))
        x_new = merge(x, gate, oa, oc, rest)
        if l < depth - 1:
            oa_x, oc_x = attend(qa_x, qc_x, ka[:, :, t:], kc[:, :, t:], va[..., t:], vc[..., t:])
            ctx = merge(ctx, gate_c, oa_x, oc_x, rest_x)
        x = x_new
    return x
```

```python
import functools
import math

import numpy as np
import jax
import jax.numpy as jnp
from jax import lax
from jax.experimental import pallas as pl
from jax.experimental.pallas import tpu as pltpu

F32 = jnp.float32
BF16 = jnp.bfloat16

D_MODEL = 1024
GRID_W = 64
ROPE_THETA = 10000.0
EPS = 1e-6
N_BRANCH = 3

A_HEAD_DIM = 64
A_WIDTH = D_MODEL // 2
A_HEADS = A_WIDTH // A_HEAD_DIM
A_KV_HEADS = A_HEADS // 4
A_GROUP = A_HEADS // A_KV_HEADS
A_KV_WIDTH = A_KV_HEADS * A_HEAD_DIM

B_WIDTH = D_MODEL // 4
B_GROUP_DIM = 64
B_GROUPS = B_WIDTH // B_GROUP_DIM
B_CHUNK = 128

C_WIDTH = D_MODEL // 4
C_QK_DIM = 32
C_V_DIM = 2 * C_QK_DIM
C_HEADS = C_WIDTH // C_V_DIM
C_QK_WIDTH = C_HEADS * 2 * C_QK_DIM

PROJ_WIDTH = 6144
R_QA, R_QC, R_KA, R_KC, R_VA, R_VC, R_REST = 0, 512, 768, 896, 1152, 1280, 1536
REST_WIDTH = PROJ_WIDTH - R_REST
S_ZA, S_ZC, S_BU, S_BV, S_BZ, S_GATES = 0, 512, 768, 1024, 1280, 1536

ACC_ROWS = 72
KEY_CHUNK = 256
PIPELINE_DEPTH = 2
QUERY_SUB = 512
SGU_BD = 2 * B_CHUNK
VMEM_LIMIT = 56 * 1024 * 1024
LOG2E = math.log2(math.e)

MIN_DENOM = 2.0 ** -80
STAB_MARGIN = 40.0


def _pick(n, candidates):
    for c in candidates:
        if n % c == 0:
            return c
    raise ValueError(f"no tile for {n} in {candidates}")


def _sigmoid(x):
    return 0.5 * jnp.tanh(0.5 * x) + 0.5


def _silu(x):
    return x * _sigmoid(x)


def _silu_of_twice(h):
    return h * (1.0 + jnp.tanh(h))


def _twice_sigmoid_of_twice(h):
    return 1.0 + jnp.tanh(h)


def _mod_kernel(c_ref, w_ref, b_ref, o_ref):
    s = _silu(c_ref[...])
    o_ref[0] = jnp.dot(s, w_ref[0], preferred_element_type=F32,
                       precision=lax.Precision.HIGHEST) + b_ref[0]


def _modulation(cs, w_mod, b_mod):
    depth = w_mod.shape[0]
    nj = 3 * D_MODEL // 1024
    return pl.pallas_call(
        _mod_kernel,
        grid=(depth, nj),
        in_specs=[pl.BlockSpec((8, D_MODEL), lambda l, j: (0, 0)),
                  pl.BlockSpec((1, D_MODEL, 1024), lambda l, j: (l, 0, j)),
                  pl.BlockSpec((1, 1, 1024), lambda l, j: (l, 0, j))],
        out_specs=pl.BlockSpec((1, 8, 1024), lambda l, j: (l, 0, j)),
        out_shape=jax.ShapeDtypeStruct((depth, 8, 3 * D_MODEL), F32),
        compiler_params=pltpu.CompilerParams(
            dimension_semantics=("parallel", "parallel"), vmem_limit_bytes=VMEM_LIMIT),
        name="adaln_mod",
    )(cs, w_mod, b_mod.reshape(depth, 1, 3 * D_MODEL))


def _norm_rope(p, g, cos, sin):
    half = p.shape[0] // 2
    ms = jnp.mean(p * p, axis=0, keepdims=True)
    y = p * lax.rsqrt(ms + EPS) * g
    x0, x1 = y[:half], y[half:]
    return jnp.concatenate([x0 * cos - x1 * sin, x0 * sin + x1 * cos], axis=0)


def _proj_kernel(*refs, n_tiles):
    i = pl.program_id(1)

    @pl.when(i < n_tiles)
    def _tile():
        _proj_tile(*refs)

    @pl.when(i >= n_tiles)
    def _tail():
        for kv_ref in refs[-6:-2]:
            kv_ref[...] = jnp.zeros_like(kv_ref)


def _proj_tile(x_ref, shift_ref, scale_ref, gn_ref, w_ref, gqa_ref, gka_ref, gqc_ref, gkc_ref,
               cosa_ref, sina_ref, cosc_ref, sinc_ref, *refs):
    qa_ref, qc_ref, ka_ref, kc_ref, va_ref, vc_ref, rest_ref, hn_ref = refs[-8:]
    x = x_ref[0]
    ms = jnp.mean(x * x, axis=1, keepdims=True)
    y = x * lax.rsqrt(ms + EPS) * gn_ref[...]
    hn_ref[...] = (y * (1.0 + scale_ref[0]) + shift_ref[0]).T.astype(BF16)

    def proj(r0, r1):
        return jnp.dot(w_ref[r0:r1, :], hn_ref[...], preferred_element_type=F32)

    cosa, sina, cosc, sinc = cosa_ref[...], sina_ref[...], cosc_ref[...], sinc_ref[...]

    pqa = proj(R_QA, R_QC)
    for h in range(A_HEADS):
        r = h * A_HEAD_DIM
        qa_ref[0, r:r + A_HEAD_DIM, :] = _norm_rope(
            pqa[r:r + A_HEAD_DIM], gqa_ref[...], cosa, sina).astype(BF16)
    pqc = proj(R_QC, R_KA)
    for u in range(2 * C_HEADS):
        r = u * C_QK_DIM
        qc_ref[0, r:r + C_QK_DIM, :] = _norm_rope(
            pqc[r:r + C_QK_DIM], gqc_ref[...], cosc, sinc).astype(BF16)
    pka = proj(R_KA, R_KC)
    for h in range(A_KV_HEADS):
        ka_ref[0, h] = _norm_rope(pka[h * A_HEAD_DIM:(h + 1) * A_HEAD_DIM], gka_ref[...],
                                  cosa, sina).T.astype(BF16)
    pkc = proj(R_KC, R_VA)
    for u in range(2 * C_HEADS):
        kc_ref[0, u] = _norm_rope(pkc[u * C_QK_DIM:(u + 1) * C_QK_DIM], gkc_ref[...],
                                  cosc, sinc).T.astype(BF16)
    va_ref[0] = proj(R_VA, R_VC).astype(BF16)
    vc_ref[0] = proj(R_VC, R_REST).astype(BF16)
    for r in range(R_REST, PROJ_WIDTH, 512):
        rest_ref[0, r - R_REST:r - R_REST + 512, :] = proj(r, r + 512).astype(BF16)


def _project(x, shift, scale, g_norm, w_t, gqa, gka, gqc, gkc, cosa, sina, cosc, sinc,
             kv_len, kv_start=0, kv_into=()):
    b, t, _ = x.shape
    tm = _pick(t, (512, 256, 128))
    assert kv_start % tm == 0
    blk0, n_tiles = kv_start // tm, t // tm
    n_tail = pl.cdiv(kv_len - kv_start - t, tm)
    tile = lambda i: jnp.minimum(i, n_tiles - 1)
    col = lambda n: pl.BlockSpec((n, 1), lambda bi, i: (0, 0))
    tok = lambda n: pl.BlockSpec((n, tm), lambda bi, i: (0, tile(i)))
    outT = lambda n: pl.BlockSpec((1, n, tm), lambda bi, i: (bi, 0, tile(i)))
    kv_tok = lambda n, d: pl.BlockSpec((1, n, tm, d), lambda bi, i: (bi, 0, i + blk0, 0))
    kv_chan = lambda n: pl.BlockSpec((1, n, tm), lambda bi, i: (bi, 0, i + blk0))
    n_in = 13
    return pl.pallas_call(
        functools.partial(_proj_kernel, n_tiles=n_tiles),
        grid=(b, n_tiles + n_tail),
        in_specs=[pl.BlockSpec((1, tm, D_MODEL), lambda bi, i: (bi, tile(i), 0)),
                  pl.BlockSpec((1, 1, D_MODEL), lambda bi, i: (bi, 0, 0)),
                  pl.BlockSpec((1, 1, D_MODEL), lambda bi, i: (bi, 0, 0)),
                  pl.BlockSpec((1, D_MODEL), lambda bi, i: (0, 0)),
                  pl.BlockSpec((PROJ_WIDTH, D_MODEL), lambda bi, i: (0, 0),
                               pipeline_mode=pl.Buffered(1)),
                  col(A_HEAD_DIM), col(A_HEAD_DIM), col(C_QK_DIM), col(C_QK_DIM),
                  tok(A_HEAD_DIM // 2), tok(A_HEAD_DIM // 2), tok(C_QK_DIM // 2), tok(C_QK_DIM // 2)]
                 + [pl.BlockSpec(memory_space=pl.ANY)] * len(kv_into),
        out_specs=[outT(A_WIDTH), outT(C_QK_WIDTH),
                   kv_tok(A_KV_HEADS, A_HEAD_DIM), kv_tok(2 * C_HEADS, C_QK_DIM),
                   kv_chan(A_KV_WIDTH), kv_chan(C_WIDTH), outT(REST_WIDTH)],
        out_shape=[jax.ShapeDtypeStruct((b, A_WIDTH, t), BF16),
                   jax.ShapeDtypeStruct((b, C_QK_WIDTH, t), BF16),
                   jax.ShapeDtypeStruct((b, A_KV_HEADS, kv_len, A_HEAD_DIM), BF16),
                   jax.ShapeDtypeStruct((b, 2 * C_HEADS, kv_len, C_QK_DIM), BF16),
                   jax.ShapeDtypeStruct((b, A_KV_WIDTH, kv_len), BF16),
                   jax.ShapeDtypeStruct((b, C_WIDTH, kv_len), BF16),
                   jax.ShapeDtypeStruct((b, REST_WIDTH, t), BF16)],
        input_output_aliases={n_in + j: 2 + j for j in range(len(kv_into))},
        scratch_shapes=[pltpu.VMEM((D_MODEL, tm), BF16)],
        compiler_params=pltpu.CompilerParams(
            dimension_semantics=("parallel", "arbitrary"), vmem_limit_bytes=VMEM_LIMIT),
        name="in_proj",
    )(x, shift, scale, g_norm, w_t, gqa, gka, gqc, gkc, cosa, sina, cosc, sinc, *kv_into)


def _attn_kernel(*refs, lam_init, online):
    if online:
        (lam_ref, gsub_ref, qa_ref, qc_ref, ka_ref, kc_ref, va_ref, vc_ref,
         oa_ref, oc_ref, acc_a, acc_c, m_a, m_c) = refs
    else:
        (stab_ref, lam_ref, gsub_ref, qa_ref, qc_ref, ka_ref, kc_ref, va_ref, vc_ref,
         oa_ref, oc_ref, lmin_ref, acc_a, acc_c) = refs
        m_a = m_c = None
    kv = pl.program_id(2)

    @pl.when(kv == 0)
    def _init():
        acc_a[...] = jnp.zeros_like(acc_a)
        acc_c[...] = jnp.zeros_like(acc_c)
        if online:
            m_a[...] = jnp.full_like(m_a, -1e30)
            m_c[...] = jnp.full_like(m_c, -1e30)

    def probs(s, shift):
        pf = jnp.exp2(s - shift)
        return pf.astype(BF16), pf.reshape(s.shape[0] // 8, 8, s.shape[1]).sum(axis=0)

    def weighted(v, p, lsum):
        return jnp.concatenate([jnp.dot(v, p, preferred_element_type=F32), lsum], axis=0)

    tk = ka_ref.shape[2]
    units = []
    for hd in range(A_HEADS):
        q = qa_ref[0, hd * A_HEAD_DIM:(hd + 1) * A_HEAD_DIM, :]
        units.append((acc_a, hd, ka_ref, hd // A_GROUP, q, hd // A_GROUP, va_ref))
    for u in range(2 * C_HEADS):
        q = qc_ref[0, u * C_QK_DIM:(u + 1) * C_QK_DIM, :]
        units.append((acc_c, u, kc_ref, u, q, u // 2, vc_ref))

    if online:
        for acc_ref, u, k_ref, kh, q, vh, v_ref in units:
            m_ref = m_a if acc_ref is acc_a else m_c
            s = jnp.dot(k_ref[0, kh], q, preferred_element_type=F32)
            m_prev = m_ref[u:u + 1, :]
            m_new = jnp.maximum(m_prev, jnp.max(s, axis=0, keepdims=True))
            m_ref[u:u + 1, :] = m_new
            acc_ref[u] = (acc_ref[u] * jnp.exp2(m_prev - m_new)
                          + weighted(v_ref[0, vh], *probs(s, m_new)))
    else:
        n_chunks = tk // KEY_CHUNK
        tq = qa_ref.shape[2]
        sub = min(tq, QUERY_SUB)
        pending, partial = [], {}

        def retire():
            ui, q0, ci, p, lsum = pending.pop(0)
            acc_ref, u, _, _, _, vh, v_ref = units[ui]
            pv = weighted(v_ref[0, vh, :, ci * KEY_CHUNK:(ci + 1) * KEY_CHUNK], p, lsum)
            partial[ui, q0] = pv if ci == 0 else partial[ui, q0] + pv
            if ci == n_chunks - 1:
                acc_ref[u, :, q0:q0 + sub] += partial.pop((ui, q0))

        for ui, (acc_ref, u, k_ref, kh, q, vh, v_ref) in enumerate(units):
            stab = stab_ref[0] if acc_ref is acc_a else stab_ref[1]
            for q0 in range(0, tq, sub):
                for ci in range(n_chunks):
                    s = jnp.dot(k_ref[0, kh, ci * KEY_CHUNK:(ci + 1) * KEY_CHUNK, :],
                                q[:, q0:q0 + sub], preferred_element_type=F32)
                    pending.append((ui, q0, ci) + probs(s, stab))
                    if len(pending) > PIPELINE_DEPTH:
                        retire()
        while pending:
            retire()

    @pl.when(kv == pl.num_programs(2) - 1)
    def _finish():
        def normalised(acc, d):
            l = jnp.sum(acc[d:], axis=0, keepdims=True)
            return acc[:d] / l, l

        lmin = None
        for hd in range(A_HEADS):
            o, l = normalised(acc_a[hd], A_HEAD_DIM)
            lmin = l if lmin is None else jnp.minimum(lmin, l)
            oa_ref[0, hd * A_HEAD_DIM:(hd + 1) * A_HEAD_DIM, :] = o.astype(BF16)
        lam_c = lam_ref[...]
        lam = (jnp.exp(jnp.sum(lam_c[0:1] * lam_c[1:2], axis=1, keepdims=True))
               - jnp.exp(jnp.sum(lam_c[2:3] * lam_c[3:4], axis=1, keepdims=True)) + lam_init)
        for h in range(C_HEADS):
            (o1, l1), (o2, l2) = (normalised(acc_c[2 * h + j], C_V_DIM) for j in range(2))
            lmin = jnp.minimum(lmin, jnp.minimum(l1, l2))
            o = o1 - lam * o2
            ms = jnp.mean(o * o, axis=0, keepdims=True)
            o = o * lax.rsqrt(ms + EPS) * gsub_ref[...] * (1.0 - lam_init)
            oc_ref[0, h * C_V_DIM:(h + 1) * C_V_DIM, :] = o.astype(BF16)
        if not online:
            lmin_ref[0, 0] = jnp.broadcast_to(lmin, lmin_ref.shape[2:])


def _attention(stab, lam_c, g_subln, qa, qc, ka, kc, va, vc, lam_init, online):
    b, _, t = qa.shape
    tkv = ka.shape[2]
    if online:
        tq, tk = _pick(t, (256, 128)), _pick(tkv, (1280, 1024, 512, 256))
    else:
        tq = _pick(t, (2 * QUERY_SUB, QUERY_SUB, 256, 128))
        tk = _pick(tkv, (3328, 1280, 1024, 512, 256))
    nq = t // tq
    in_specs = [pl.BlockSpec((4, C_QK_DIM), lambda bi, i, j: (0, 0)),
                pl.BlockSpec((C_V_DIM, 1), lambda bi, i, j: (0, 0)),
                pl.BlockSpec((1, A_WIDTH, tq), lambda bi, i, j: (bi, 0, i)),
                pl.BlockSpec((1, C_QK_WIDTH, tq), lambda bi, i, j: (bi, 0, i)),
                pl.BlockSpec((1, A_KV_HEADS, tk, A_HEAD_DIM), lambda bi, i, j: (bi, 0, j, 0)),
                pl.BlockSpec((1, 2 * C_HEADS, tk, C_QK_DIM), lambda bi, i, j: (bi, 0, j, 0)),
                pl.BlockSpec((1, A_KV_HEADS, A_HEAD_DIM, tk), lambda bi, i, j: (bi, 0, 0, j)),
                pl.BlockSpec((1, C_HEADS, C_V_DIM, tk), lambda bi, i, j: (bi, 0, 0, j))]
    out_specs = [pl.BlockSpec((1, A_WIDTH, tq), lambda bi, i, j: (bi, 0, i)),
                 pl.BlockSpec((1, C_WIDTH, tq), lambda bi, i, j: (bi, 0, i))]
    out_shape = [jax.ShapeDtypeStruct((b, A_WIDTH, t), BF16),
                 jax.ShapeDtypeStruct((b, C_WIDTH, t), BF16)]
    scratch = [pltpu.VMEM((A_HEADS, ACC_ROWS, tq), F32),
               pltpu.VMEM((2 * C_HEADS, ACC_ROWS, tq), F32)]
    args = (lam_c, g_subln, qa, qc, ka, kc, va, vc)
    if online:
        scratch += [pltpu.VMEM((A_HEADS, tq), F32), pltpu.VMEM((2 * C_HEADS, tq), F32)]
    else:
        in_specs = [pl.BlockSpec(memory_space=pltpu.SMEM)] + in_specs
        out_specs.append(pl.BlockSpec((1, 1, 8, tq), lambda bi, i, j: (bi, i, 0, 0)))
        out_shape.append(jax.ShapeDtypeStruct((b, nq, 8, tq), F32))
        args = (stab,) + args
    return pl.pallas_call(
        functools.partial(_attn_kernel, lam_init=lam_init, online=online),
        grid=(b, nq, tkv // tk),
        in_specs=in_specs, out_specs=out_specs, out_shape=out_shape, scratch_shapes=scratch,
        compiler_params=pltpu.CompilerParams(
            dimension_semantics=("parallel", "parallel", "arbitrary"),
            vmem_limit_bytes=VMEM_LIMIT),
        name="attention_online" if online else "attention_fixed",
    )(*args)


def _attend(gains_a, gains_c, lam_c, g_subln, qa, qc, ka, kc, va, vc, lam_init):
    bound = lambda gains, d: (1.02 * LOG2E * d ** 0.5
                              * jnp.max(jnp.abs(gains[0])) * jnp.max(jnp.abs(gains[1])))
    stab = jnp.stack([bound(gains_a, A_HEAD_DIM), bound(gains_c, C_QK_DIM)]) - STAB_MARGIN
    args = (lam_c, g_subln, qa, qc, ka, kc, va, vc, lam_init)
    oa, oc, lmin = _attention(stab.astype(F32), *args, online=False)
    return lax.cond(jnp.all(lmin >= MIN_DENOM), lambda: (oa, oc),
                    lambda: tuple(_attention(None, *args, online=True)))


def _merge_kernel(x_ref, gate_ref, oa_ref, oc_ref, rest_ref, gsgu_ref, wsp_ref, bsp_ref,
                  wpa_ref, wpb_ref, wpc_ref, wout_ref, o_ref, mt_ref):
    tm = x_ref.shape[1]
    rest = lambda r0, n: rest_ref[0, r0:r0 + n, :].astype(F32)

    mixed_groups = []
    for g in range(B_GROUPS):
        v = rest(S_BV + g * B_GROUP_DIM, B_GROUP_DIM)
        ms = jnp.mean(v * v, axis=0, keepdims=True)
        vn = (v * lax.rsqrt(ms + EPS) * gsgu_ref[g]).astype(BF16)
        cols = [jnp.dot(vn[:, c:c + SGU_BD], wsp_ref[g], preferred_element_type=F32) + bsp_ref[g]
                for c in range(0, tm, SGU_BD)]
        mixed_groups.append(cols[0] if len(cols) == 1 else jnp.concatenate(cols, axis=1))
    ha = (oa_ref[0].astype(F32) * _silu_of_twice(rest(S_ZA, A_WIDTH))).astype(BF16)
    ya = jnp.dot(wpa_ref[...], ha, preferred_element_type=F32)
    hc = (oc_ref[0].astype(F32) * _silu_of_twice(rest(S_ZC, C_WIDTH))).astype(BF16)
    yc = jnp.dot(wpc_ref[...], hc, preferred_element_type=F32)
    mixed = jnp.concatenate(mixed_groups, axis=0)
    hb = (rest(S_BU, B_WIDTH) * mixed * _silu_of_twice(rest(S_BZ, B_WIDTH))).astype(BF16)
    yb = jnp.dot(wpb_ref[...], hb, preferred_element_type=F32)

    m = (_twice_sigmoid_of_twice(rest(S_GATES, D_MODEL)) * ya
         + _twice_sigmoid_of_twice(rest(S_GATES + D_MODEL, D_MODEL)) * yb
         + _twice_sigmoid_of_twice(rest(S_GATES + 2 * D_MODEL, D_MODEL)) * yc)
    mt_ref[...] = m.T.astype(BF16)
    out = jnp.dot(mt_ref[...], wout_ref[...], preferred_element_type=F32)
    o_ref[0] = x_ref[0] + gate_ref[0] * out


def _merge(x, gate, oa, oc, rest, gsgu, wsp_bd, bsp, wpa_t, wpb_t, wpc_t, wout):
    b, t, _ = x.shape
    tm = _pick(t, (512, 256))
    tokT = lambda n: pl.BlockSpec((1, n, tm), lambda bi, i: (bi, 0, i))
    full = lambda shape: pl.BlockSpec(shape, lambda bi, i: (0,) * len(shape))
    return pl.pallas_call(
        _merge_kernel,
        grid=(b, t // tm),
        in_specs=[pl.BlockSpec((1, tm, D_MODEL), lambda bi, i: (bi, i, 0)),
                  pl.BlockSpec((1, 1, D_MODEL), lambda bi, i: (bi, 0, 0)),
                  tokT(A_WIDTH), tokT(C_WIDTH), tokT(REST_WIDTH),
                  full((B_GROUPS, B_GROUP_DIM, 1)), full((B_GROUPS, SGU_BD, SGU_BD)),
                  full((B_GROUPS, 1, SGU_BD)),
                  full((D_MODEL, A_WIDTH)), full((D_MODEL, B_WIDTH)), full((D_MODEL, C_WIDTH)),
                  full((D_MODEL, D_MODEL))],
        out_specs=pl.BlockSpec((1, tm, D_MODEL), lambda bi, i: (bi, i, 0)),
        out_shape=jax.ShapeDtypeStruct((b, t, D_MODEL), F32),
        scratch_shapes=[pltpu.VMEM((tm, D_MODEL), BF16)],
        compiler_params=pltpu.CompilerParams(
            dimension_semantics=("parallel", "parallel"), vmem_limit_bytes=VMEM_LIMIT),
        name="merge_out",
    )(x, gate, oa, oc, rest, gsgu, wsp_bd, bsp, wpa_t, wpb_t, wpc_t, wout)


def _deinterleaved(w, n_heads):
    dim = w.shape[0] // n_heads
    w = w.reshape(n_heads, dim // 2, 2, *w.shape[1:])
    return jnp.swapaxes(w, 1, 2).reshape(n_heads * dim, *w.shape[3:])


def _projection_rows(w_in):
    off = np.cumsum([0, A_WIDTH, C_QK_WIDTH, A_KV_WIDTH, A_KV_WIDTH, C_QK_WIDTH, C_WIDTH])
    w = w_in.T
    qa, qc, ka, va, kc, vc = (w[off[i]:off[i + 1]] for i in range(6))
    half = np.ones((REST_WIDTH, 1), np.float32)
    half[S_ZA:S_BU] = 0.5
    half[S_BZ:] = 0.5
    return jnp.concatenate(
        [_deinterleaved(qa, A_HEADS), _deinterleaved(qc, 2 * C_HEADS),
         _deinterleaved(ka, A_KV_HEADS), _deinterleaved(kc, 2 * C_HEADS), va, vc,
         w[off[6]:] * half], axis=0).astype(BF16)


def _rope_tables(t, dim):
    quarter = dim // 4
    tok = jnp.arange(t)
    inv = jnp.power(ROPE_THETA, -jnp.arange(quarter, dtype=F32) / quarter)
    ang = jnp.concatenate([(tok // GRID_W)[:, None].astype(F32) * inv,
                           (tok % GRID_W)[:, None].astype(F32) * inv], axis=-1)
    return jnp.cos(ang).T, jnp.sin(ang).T


def kernel(x, c, ctx, c_ctx, w_mod, b_mod, g_norm, w_in, gq_a, gk_a, gq_c, gk_c, g_sgu, w_sp,
           b_sp, lam_c, g_subln, w_pa, w_pb, w_pc, w_out):
    depth = w_in.shape[0]
    b, t, _ = x.shape
    n_ctx = ctx.shape[1]

    cs = jnp.zeros((8, D_MODEL), F32).at[:b].set(c).at[b].set(c_ctx)
    mod = _modulation(cs, w_mod, b_mod)

    cosa, sina = _rope_tables(t, A_HEAD_DIM)
    cosc, sinc = _rope_tables(t, C_QK_DIM)
    one = lambda n: jnp.ones((n, n_ctx), F32)
    zero = lambda n: jnp.zeros((n, n_ctx), F32)
    eye2 = jnp.eye(SGU_BD // B_CHUNK, dtype=F32)

    for l in range(depth):
        lam_init = 0.8 - 0.6 * math.exp(-0.3 * l)
        rowv = lambda v: v.reshape(-1, 1, D_MODEL)
        shift, scale, gate = (rowv(mod[l, :b, i * D_MODEL:(i + 1) * D_MODEL]) for i in range(3))
        shift_c, scale_c, gate_c = (
            rowv(jnp.broadcast_to(mod[l, b, i * D_MODEL:(i + 1) * D_MODEL], (b, D_MODEL)))
            for i in range(3))
        colv = lambda g: _deinterleaved(g, 1).reshape(-1, 1)
        proj = functools.partial(
            _project, g_norm=g_norm[l].reshape(1, D_MODEL), w_t=_projection_rows(w_in[l]),
            gqa=colv(gq_a[l] * (A_HEAD_DIM ** -0.5 * LOG2E)), gka=colv(gk_a[l]),
            gqc=colv(gq_c[l] * (C_QK_DIM ** -0.5 * LOG2E)), gkc=colv(gk_c[l]))

        qa, qc, ka, kc, va, vc, rest = proj(
            x, shift, scale, cosa=cosa, sina=sina, cosc=cosc, sinc=sinc, kv_len=t + n_ctx)
        qa_x, qc_x, ka, kc, va, vc, rest_x = proj(
            ctx, shift_c, scale_c, cosa=one(A_HEAD_DIM // 2), sina=zero(A_HEAD_DIM // 2),
            cosc=one(C_QK_DIM // 2), sinc=zero(C_QK_DIM // 2),
            kv_len=t + n_ctx, kv_start=t, kv_into=(ka, kc, va, vc))
        va, vc = va.reshape(b, A_KV_HEADS, -1, t + n_ctx), vc.reshape(b, C_HEADS, -1, t + n_ctx)

        attend = functools.partial(
            _attend, (gq_a[l], gk_a[l]), (gq_c[l], gk_c[l]), lam_c[l],
            g_subln[l].reshape(C_V_DIM, 1), lam_init=lam_init)
        oa, oc = attend(qa, qc, ka, kc, va, vc)

        wsp_bd = jnp.einsum("ij,gqp->giqjp", eye2, jnp.swapaxes(w_sp[l], 1, 2)).reshape(
            B_GROUPS, SGU_BD, SGU_BD).astype(BF16)
        bsp = jnp.tile(b_sp[l], (1, SGU_BD // B_CHUNK)).reshape(B_GROUPS, 1, SGU_BD)
        merge = functools.partial(
            _merge, gsgu=g_sgu[l].reshape(B_GROUPS, B_GROUP_DIM, 1), wsp_bd=wsp_bd, bsp=bsp,
            wpa_t=w_pa[l].T.astype(BF16), wpb_t=w_pb[l].T.astype(BF16),
            wpc_t=w_pc[l].T.astype(BF16), wout=(0.5 * w_out[l]).astype(BF16))
        x_new = merge(x, gate, oa, oc, rest)
        if l < depth - 1:
            oa_x, oc_x = attend(qa_x, qc_x, ka[:, :, t:], kc[:, :, t:], va[..., t:], vc[..., t:])
            ctx = merge(ctx, gate_c, oa_x, oc_x, rest_x)
        x = x_new
    return x
```

```python
import functools
import math

import numpy as np
import jax
import jax.numpy as jnp
from jax import lax
from jax.experimental import pallas as pl
from jax.experimental.pallas import tpu as pltpu

F32 = jnp.float32
BF16 = jnp.bfloat16

D_MODEL = 1024
GRID_W = 64
ROPE_THETA = 10000.0
EPS = 1e-6
N_BRANCH = 3

A_HEAD_DIM = 64
A_WIDTH = D_MODEL // 2
A_HEADS = A_WIDTH // A_HEAD_DIM
A_KV_HEADS = A_HEADS // 4
A_GROUP = A_HEADS // A_KV_HEADS
A_KV_WIDTH = A_KV_HEADS * A_HEAD_DIM

B_WIDTH = D_MODEL // 4
B_GROUP_DIM = 64
B_GROUPS = B_WIDTH // B_GROUP_DIM
B_CHUNK = 128

C_WIDTH = D_MODEL // 4
C_QK_DIM = 32
C_V_DIM = 2 * C_QK_DIM
C_HEADS = C_WIDTH // C_V_DIM
C_QK_WIDTH = C_HEADS * 2 * C_QK_DIM

R_QA, R_QC, R_KA, R_KC, R_VA, R_VC, R_REST = (int(o) for o in np.cumsum(
    [0, A_WIDTH, C_QK_WIDTH, A_KV_WIDTH, C_QK_WIDTH, A_KV_WIDTH, C_WIDTH]))
S_ZA, S_ZC, S_BU, S_BV, S_BZ, S_GATES, REST_WIDTH = (int(o) for o in np.cumsum(
    [0, A_WIDTH, C_WIDTH, B_WIDTH, B_WIDTH, B_WIDTH, N_BRANCH * D_MODEL]))
PROJ_WIDTH = R_REST + REST_WIDTH

SUBLANES = 8
MXU_DIM = 256
VMEM_LIMIT = 56 * 1024 * 1024

ACC_ROWS = C_V_DIM + SUBLANES
KEY_CHUNK = MXU_DIM
PIPELINE_DEPTH = 2
SGU_BD = 2 * B_CHUNK
MOD_COLS = 1024
TOKEN_TILES = (512, 256, 128)
QUERY_TILES = (2 * MXU_DIM, MXU_DIM, 128)
KEY_TILES = (5 * KEY_CHUNK, 4 * KEY_CHUNK, 2 * KEY_CHUNK, KEY_CHUNK)
FALLBACK_QUERY_TILES = (MXU_DIM, 128)
FALLBACK_KEY_TILES = KEY_TILES
LOG2E = math.log2(math.e)

MIN_DENOM = 2.0 ** -80
STAB_MARGIN = 40.0


def _pick(n, candidates):
    for c in candidates:
        if n % c == 0:
            return c
    raise ValueError(f"no tile for {n} in {candidates}")


def _sigmoid(x):
    return 0.5 * jnp.tanh(0.5 * x) + 0.5


def _silu(x):
    return x * _sigmoid(x)


def _silu_of_twice(h):
    return h * (1.0 + jnp.tanh(h))


def _twice_sigmoid_of_twice(h):
    return 1.0 + jnp.tanh(h)


def _mod_kernel(c_ref, w_ref, b_ref, o_ref):
    s = _silu(c_ref[...])
    o_ref[0] = jnp.dot(s, w_ref[0], preferred_element_type=F32,
                       precision=lax.Precision.HIGHEST) + b_ref[0]


def _modulation(cs, w_mod, b_mod):
    depth = w_mod.shape[0]
    rows = cs.shape[0]
    return pl.pallas_call(
        _mod_kernel,
        grid=(depth, 3 * D_MODEL // MOD_COLS),
        in_specs=[pl.BlockSpec((rows, D_MODEL), lambda l, j: (0, 0)),
                  pl.BlockSpec((1, D_MODEL, MOD_COLS), lambda l, j: (l, 0, j)),
                  pl.BlockSpec((1, 1, MOD_COLS), lambda l, j: (l, 0, j))],
        out_specs=pl.BlockSpec((1, rows, MOD_COLS), lambda l, j: (l, 0, j)),
        out_shape=jax.ShapeDtypeStruct((depth, rows, 3 * D_MODEL), F32),
        compiler_params=pltpu.CompilerParams(
            dimension_semantics=("parallel", "parallel"), vmem_limit_bytes=VMEM_LIMIT),
        name="adaln_mod",
    )(cs, w_mod, b_mod.reshape(depth, 1, 3 * D_MODEL))


def _norm_rope(p, g, cos, sin):
    half = p.shape[0] // 2
    ms = jnp.mean(p * p, axis=0, keepdims=True)
    y = p * lax.rsqrt(ms + EPS) * g
    x0, x1 = y[:half], y[half:]
    return jnp.concatenate([x0 * cos - x1 * sin, x0 * sin + x1 * cos], axis=0)


def _proj_kernel(*refs, n_tiles):
    i = pl.program_id(1)

    @pl.when(i < n_tiles)
    def _tile():
        _proj_tile(*refs)

    @pl.when(i >= n_tiles)
    def _tail():
        for kv_ref in refs[-6:-2]:
            kv_ref[...] = jnp.zeros_like(kv_ref)


def _proj_tile(x_ref, shift_ref, scale_ref, gn_ref, w_ref, gqa_ref, gka_ref, gqc_ref, gkc_ref,
               cosa_ref, sina_ref, cosc_ref, sinc_ref, *refs):
    qa_ref, qc_ref, ka_ref, kc_ref, va_ref, vc_ref, rest_ref, hn_ref = refs[-8:]
    x = x_ref[0]
    ms = jnp.mean(x * x, axis=1, keepdims=True)
    y = x * lax.rsqrt(ms + EPS) * gn_ref[...]
    hn_ref[...] = (y * (1.0 + scale_ref[0]) + shift_ref[0]).T.astype(BF16)

    def proj(r0, r1):
        return jnp.dot(w_ref[r0:r1, :], hn_ref[...], preferred_element_type=F32)

    cosa, sina, cosc, sinc = cosa_ref[...], sina_ref[...], cosc_ref[...], sinc_ref[...]

    pqa = proj(R_QA, R_QC)
    for h in range(A_HEADS):
        r = h * A_HEAD_DIM
        qa_ref[0, r:r + A_HEAD_DIM, :] = _norm_rope(
            pqa[r:r + A_HEAD_DIM], gqa_ref[...], cosa, sina).astype(BF16)
    pqc = proj(R_QC, R_KA)
    for u in range(2 * C_HEADS):
        r = u * C_QK_DIM
        qc_ref[0, r:r + C_QK_DIM, :] = _norm_rope(
            pqc[r:r + C_QK_DIM], gqc_ref[...], cosc, sinc).astype(BF16)
    pka = proj(R_KA, R_KC)
    for h in range(A_KV_HEADS):
        ka_ref[0, h] = _norm_rope(pka[h * A_HEAD_DIM:(h + 1) * A_HEAD_DIM], gka_ref[...],
                                  cosa, sina).T.astype(BF16)
    pkc = proj(R_KC, R_VA)
    for u in range(2 * C_HEADS):
        kc_ref[0, u] = _norm_rope(pkc[u * C_QK_DIM:(u + 1) * C_QK_DIM], gkc_ref[...],
                                  cosc, sinc).T.astype(BF16)
    va_ref[0] = proj(R_VA, R_VC).astype(BF16)
    vc_ref[0] = proj(R_VC, R_REST).astype(BF16)
    for r in range(R_REST, PROJ_WIDTH, 512):
        rest_ref[0, r - R_REST:r - R_REST + 512, :] = proj(r, r + 512).astype(BF16)


def _project(x, shift, scale, g_norm, w_t, gqa, gka, gqc, gkc, cosa, sina, cosc, sinc,
             kv_len, kv_start=0, kv_into=()):
    b, t, _ = x.shape
    tm = _pick(t, TOKEN_TILES)
    assert kv_start % tm == 0
    blk0, n_tiles = kv_start // tm, t // tm
    n_tail = pl.cdiv(kv_len - kv_start - t, tm)
    tile = lambda i: jnp.minimum(i, n_tiles - 1)
    col = lambda n: pl.BlockSpec((n, 1), lambda bi, i: (0, 0))
    tok = lambda n: pl.BlockSpec((n, tm), lambda bi, i: (0, tile(i)))
    outT = lambda n: pl.BlockSpec((1, n, tm), lambda bi, i: (bi, 0, tile(i)))
    kv_tok = lambda n, d: pl.BlockSpec((1, n, tm, d), lambda bi, i: (bi, 0, i + blk0, 0))
    kv_chan = lambda n: pl.BlockSpec((1, n, tm), lambda bi, i: (bi, 0, i + blk0))
    n_in = 13
    return pl.pallas_call(
        functools.partial(_proj_kernel, n_tiles=n_tiles),
        grid=(b, n_tiles + n_tail),
        in_specs=[pl.BlockSpec((1, tm, D_MODEL), lambda bi, i: (bi, tile(i), 0)),
                  pl.BlockSpec((1, 1, D_MODEL), lambda bi, i: (bi, 0, 0)),
                  pl.BlockSpec((1, 1, D_MODEL), lambda bi, i: (bi, 0, 0)),
                  pl.BlockSpec((1, D_MODEL), lambda bi, i: (0, 0)),
                  pl.BlockSpec((PROJ_WIDTH, D_MODEL), lambda bi, i: (0, 0),
                               pipeline_mode=pl.Buffered(1)),
                  col(A_HEAD_DIM), col(A_HEAD_DIM), col(C_QK_DIM), col(C_QK_DIM),
                  tok(A_HEAD_DIM // 2), tok(A_HEAD_DIM // 2), tok(C_QK_DIM // 2), tok(C_QK_DIM // 2)]
                 + [pl.BlockSpec(memory_space=pl.ANY)] * len(kv_into),
        out_specs=[outT(A_WIDTH), outT(C_QK_WIDTH),
                   kv_tok(A_KV_HEADS, A_HEAD_DIM), kv_tok(2 * C_HEADS, C_QK_DIM),
                   kv_chan(A_KV_WIDTH), kv_chan(C_WIDTH), outT(REST_WIDTH)],
        out_shape=[jax.ShapeDtypeStruct((b, A_WIDTH, t), BF16),
                   jax.ShapeDtypeStruct((b, C_QK_WIDTH, t), BF16),
                   jax.ShapeDtypeStruct((b, A_KV_HEADS, kv_len, A_HEAD_DIM), BF16),
                   jax.ShapeDtypeStruct((b, 2 * C_HEADS, kv_len, C_QK_DIM), BF16),
                   jax.ShapeDtypeStruct((b, A_KV_WIDTH, kv_len), BF16),
                   jax.ShapeDtypeStruct((b, C_WIDTH, kv_len), BF16),
                   jax.ShapeDtypeStruct((b, REST_WIDTH, t), BF16)],
        input_output_aliases={n_in + j: 2 + j for j in range(len(kv_into))},
        scratch_shapes=[pltpu.VMEM((D_MODEL, tm), BF16)],
        compiler_params=pltpu.CompilerParams(
            dimension_semantics=("parallel", "arbitrary"), vmem_limit_bytes=VMEM_LIMIT),
        name="in_proj",
    )(x, shift, scale, g_norm, w_t, gqa, gka, gqc, gkc, cosa, sina, cosc, sinc, *kv_into)


def _attn_kernel(*refs, lam_init, online):
    if online:
        (lam_ref, gsub_ref, qa_ref, qc_ref, ka_ref, kc_ref, va_ref, vc_ref,
         oa_ref, oc_ref, acc_a, acc_c, m_a, m_c) = refs
    else:
        (stab_ref, lam_ref, gsub_ref, qa_ref, qc_ref, ka_ref, kc_ref, va_ref, vc_ref,
         oa_ref, oc_ref, lmin_ref, acc_a, acc_c) = refs
        m_a = m_c = None
    kv = pl.program_id(2)

    @pl.when(kv == 0)
    def _init():
        acc_a[...] = jnp.zeros_like(acc_a)
        acc_c[...] = jnp.zeros_like(acc_c)
        if online:
            m_a[...] = jnp.full_like(m_a, -1e30)
            m_c[...] = jnp.full_like(m_c, -1e30)

    def probs(s, shift):
        pf = jnp.exp2(s - shift)
        return pf.astype(BF16), pf.reshape(-1, SUBLANES, s.shape[1]).sum(axis=0)

    def weighted(v, p, lsum):
        return jnp.concatenate([jnp.dot(v, p, preferred_element_type=F32), lsum], axis=0)

    tk = ka_ref.shape[2]
    units = []
    for hd in range(A_HEADS):
        q = qa_ref[0, hd * A_HEAD_DIM:(hd + 1) * A_HEAD_DIM, :]
        units.append((acc_a, hd, ka_ref, hd // A_GROUP, q, hd // A_GROUP, va_ref))
    for u in range(2 * C_HEADS):
        q = qc_ref[0, u * C_QK_DIM:(u + 1) * C_QK_DIM, :]
        units.append((acc_c, u, kc_ref, u, q, u // 2, vc_ref))

    if online:
        for acc_ref, u, k_ref, kh, q, vh, v_ref in units:
            m_ref = m_a if acc_ref is acc_a else m_c
            s = jnp.dot(k_ref[0, kh], q, preferred_element_type=F32)
            m_prev = m_ref[u:u + 1, :]
            m_new = jnp.maximum(m_prev, jnp.max(s, axis=0, keepdims=True))
            m_ref[u:u + 1, :] = m_new
            acc_ref[u] = (acc_ref[u] * jnp.exp2(m_prev - m_new)
                          + weighted(v_ref[0, vh], *probs(s, m_new)))
    else:
        n_chunks = tk // KEY_CHUNK
        pending, partial = [], {}

        def retire():
            ui, ci, p, lsum = pending.pop(0)
            acc_ref, u, _, _, _, vh, v_ref = units[ui]
            pv = weighted(v_ref[0, vh, :, ci * KEY_CHUNK:(ci + 1) * KEY_CHUNK], p, lsum)
            partial[ui] = pv if ci == 0 else partial[ui] + pv
            if ci == n_chunks - 1:
                acc_ref[u] += partial.pop(ui)

        for ui, (acc_ref, u, k_ref, kh, q, vh, v_ref) in enumerate(units):
            stab = stab_ref[0] if acc_ref is acc_a else stab_ref[1]
            for ci in range(n_chunks):
                s = jnp.dot(k_ref[0, kh, ci * KEY_CHUNK:(ci + 1) * KEY_CHUNK, :], q,
                            preferred_element_type=F32)
                pending.append((ui, ci) + probs(s, stab))
                if len(pending) > PIPELINE_DEPTH:
                    retire()
        while pending:
            retire()

    @pl.when(kv == pl.num_programs(2) - 1)
    def _finish():
        def normalised(acc, d):
            l = jnp.sum(acc[d:], axis=0, keepdims=True)
            return acc[:d] / l, l

        lmin = None
        for hd in range(A_HEADS):
            o, l = normalised(acc_a[hd], A_HEAD_DIM)
            lmin = l if lmin is None else jnp.minimum(lmin, l)
            oa_ref[0, hd * A_HEAD_DIM:(hd + 1) * A_HEAD_DIM, :] = o.astype(BF16)
        lam_c = lam_ref[...]
        lam = (jnp.exp(jnp.sum(lam_c[0:1] * lam_c[1:2], axis=1, keepdims=True))
               - jnp.exp(jnp.sum(lam_c[2:3] * lam_c[3:4], axis=1, keepdims=True)) + lam_init)
        for h in range(C_HEADS):
            (o1, l1), (o2, l2) = (normalised(acc_c[2 * h + j], C_V_DIM) for j in range(2))
            lmin = jnp.minimum(lmin, jnp.minimum(l1, l2))
            o = o1 - lam * o2
            ms = jnp.mean(o * o, axis=0, keepdims=True)
            o = o * lax.rsqrt(ms + EPS) * gsub_ref[...] * (1.0 - lam_init)
            oc_ref[0, h * C_V_DIM:(h + 1) * C_V_DIM, :] = o.astype(BF16)
        if not online:
            lmin_ref[0, 0] = jnp.broadcast_to(lmin, lmin_ref.shape[2:])


def _attention(stab, lam_c, g_subln, qa, qc, ka, kc, va, vc, lam_init, online):
    b, _, t = qa.shape
    tkv = ka.shape[2]
    if online:
        tq, tk = _pick(t, FALLBACK_QUERY_TILES), _pick(tkv, FALLBACK_KEY_TILES)
    else:
        tq, tk = _pick(t, QUERY_TILES), _pick(tkv, KEY_TILES)
    nq = t // tq
    in_specs = [pl.BlockSpec((4, C_QK_DIM), lambda bi, i, j: (0, 0)),
                pl.BlockSpec((C_V_DIM, 1), lambda bi, i, j: (0, 0)),
                pl.BlockSpec((1, A_WIDTH, tq), lambda bi, i, j: (bi, 0, i)),
                pl.BlockSpec((1, C_QK_WIDTH, tq), lambda bi, i, j: (bi, 0, i)),
                pl.BlockSpec((1, A_KV_HEADS, tk, A_HEAD_DIM), lambda bi, i, j: (bi, 0, j, 0)),
                pl.BlockSpec((1, 2 * C_HEADS, tk, C_QK_DIM), lambda bi, i, j: (bi, 0, j, 0)),
                pl.BlockSpec((1, A_KV_HEADS, A_HEAD_DIM, tk), lambda bi, i, j: (bi, 0, 0, j)),
                pl.BlockSpec((1, C_HEADS, C_V_DIM, tk), lambda bi, i, j: (bi, 0, 0, j))]
    out_specs = [pl.BlockSpec((1, A_WIDTH, tq), lambda bi, i, j: (bi, 0, i)),
                 pl.BlockSpec((1, C_WIDTH, tq), lambda bi, i, j: (bi, 0, i))]
    out_shape = [jax.ShapeDtypeStruct((b, A_WIDTH, t), BF16),
                 jax.ShapeDtypeStruct((b, C_WIDTH, t), BF16)]
    scratch = [pltpu.VMEM((A_HEADS, ACC_ROWS, tq), F32),
               pltpu.VMEM((2 * C_HEADS, ACC_ROWS, tq), F32)]
    args = (lam_c, g_subln, qa, qc, ka, kc, va, vc)
    if online:
        scratch += [pltpu.VMEM((A_HEADS, tq), F32), pltpu.VMEM((2 * C_HEADS, tq), F32)]
    else:
        in_specs = [pl.BlockSpec(memory_space=pltpu.SMEM)] + in_specs
        out_specs.append(pl.BlockSpec((1, 1, SUBLANES, tq), lambda bi, i, j: (bi, i, 0, 0)))
        out_shape.append(jax.ShapeDtypeStruct((b, nq, SUBLANES, tq), F32))
        args = (stab,) + args
    return pl.pallas_call(
        functools.partial(_attn_kernel, lam_init=lam_init, online=online),
        grid=(b, nq, tkv // tk),
        in_specs=in_specs, out_specs=out_specs, out_shape=out_shape, scratch_shapes=scratch,
        compiler_params=pltpu.CompilerParams(
            dimension_semantics=("parallel", "parallel", "arbitrary"),
            vmem_limit_bytes=VMEM_LIMIT),
        name="attention_online" if online else "attention_fixed",
    )(*args)


def _attend(gains_a, gains_c, lam_c, g_subln, qa, qc, ka, kc, va, vc, lam_init):
    bound = lambda gains, d: (1.02 * LOG2E * d ** 0.5
                              * jnp.max(jnp.abs(gains[0])) * jnp.max(jnp.abs(gains[1])))
    stab = jnp.stack([bound(gains_a, A_HEAD_DIM), bound(gains_c, C_QK_DIM)]) - STAB_MARGIN
    args = (lam_c, g_subln, qa, qc, ka, kc, va, vc, lam_init)
    oa, oc, lmin = _attention(stab.astype(F32), *args, online=False)
    return lax.cond(jnp.all(lmin >= MIN_DENOM), lambda: (oa, oc),
                    lambda: tuple(_attention(None, *args, online=True)))


def _merge_kernel(x_ref, gate_ref, oa_ref, oc_ref, rest_ref, gsgu_ref, wsp_ref, bsp_ref,
                  wpa_ref, wpb_ref, wpc_ref, wout_ref, o_ref, mt_ref):
    tm = x_ref.shape[1]
    rest = lambda r0, n: rest_ref[0, r0:r0 + n, :].astype(F32)

    mixed_groups = []
    for g in range(B_GROUPS):
        v = rest(S_BV + g * B_GROUP_DIM, B_GROUP_DIM)
        ms = jnp.mean(v * v, axis=0, keepdims=True)
        vn = (v * lax.rsqrt(ms + EPS) * gsgu_ref[g]).astype(BF16)
        cols = [jnp.dot(vn[:, c:c + SGU_BD], wsp_ref[g], preferred_element_type=F32) + bsp_ref[g]
                for c in range(0, tm, SGU_BD)]
        mixed_groups.append(cols[0] if len(cols) == 1 else jnp.concatenate(cols, axis=1))
    ha = (oa_ref[0].astype(F32) * _silu_of_twice(rest(S_ZA, A_WIDTH))).astype(BF16)
    ya = jnp.dot(wpa_ref[...], ha, preferred_element_type=F32)
    hc = (oc_ref[0].astype(F32) * _silu_of_twice(rest(S_ZC, C_WIDTH))).astype(BF16)
    yc = jnp.dot(wpc_ref[...], hc, preferred_element_type=F32)
    mixed = jnp.concatenate(mixed_groups, axis=0)
    hb = (rest(S_BU, B_WIDTH) * mixed * _silu_of_twice(rest(S_BZ, B_WIDTH))).astype(BF16)
    yb = jnp.dot(wpb_ref[...], hb, preferred_element_type=F32)

    m = (_twice_sigmoid_of_twice(rest(S_GATES, D_MODEL)) * ya
         + _twice_sigmoid_of_twice(rest(S_GATES + D_MODEL, D_MODEL)) * yb
         + _twice_sigmoid_of_twice(rest(S_GATES + 2 * D_MODEL, D_MODEL)) * yc)
    mt_ref[...] = m.T.astype(BF16)
    out = jnp.dot(mt_ref[...], wout_ref[...], preferred_element_type=F32)
    o_ref[0] = x_ref[0] + gate_ref[0] * out


def _merge(x, gate, oa, oc, rest, gsgu, wsp_bd, bsp, wpa_t, wpb_t, wpc_t, wout):
    b, t, _ = x.shape
    tm = _pick(t, TOKEN_TILES[:2])
    tokT = lambda n: pl.BlockSpec((1, n, tm), lambda bi, i: (bi, 0, i))
    full = lambda shape: pl.BlockSpec(shape, lambda bi, i: (0,) * len(shape))
    return pl.pallas_call(
        _merge_kernel,
        grid=(b, t // tm),
        in_specs=[pl.BlockSpec((1, tm, D_MODEL), lambda bi, i: (bi, i, 0)),
                  pl.BlockSpec((1, 1, D_MODEL), lambda bi, i: (bi, 0, 0)),
                  tokT(A_WIDTH), tokT(C_WIDTH), tokT(REST_WIDTH),
                  full((B_GROUPS, B_GROUP_DIM, 1)), full((B_GROUPS, SGU_BD, SGU_BD)),
                  full((B_GROUPS, 1, SGU_BD)),
                  full((D_MODEL, A_WIDTH)), full((D_MODEL, B_WIDTH)), full((D_MODEL, C_WIDTH)),
                  full((D_MODEL, D_MODEL))],
        out_specs=pl.BlockSpec((1, tm, D_MODEL), lambda bi, i: (bi, i, 0)),
        out_shape=jax.ShapeDtypeStruct((b, t, D_MODEL), F32),
        scratch_shapes=[pltpu.VMEM((tm, D_MODEL), BF16)],
        compiler_params=pltpu.CompilerParams(
            dimension_semantics=("parallel", "parallel"), vmem_limit_bytes=VMEM_LIMIT),
        name="merge_out",
    )(x, gate, oa, oc, rest, gsgu, wsp_bd, bsp, wpa_t, wpb_t, wpc_t, wout)


def _deinterleaved(w, n_heads):
    dim = w.shape[0] // n_heads
    w = w.reshape(n_heads, dim // 2, 2, *w.shape[1:])
    return jnp.swapaxes(w, 1, 2).reshape(n_heads * dim, *w.shape[3:])


def _projection_rows(w_in):
    off = np.cumsum([0, A_WIDTH, C_QK_WIDTH, A_KV_WIDTH, A_KV_WIDTH, C_QK_WIDTH, C_WIDTH])
    w = w_in.T
    qa, qc, ka, va, kc, vc = (w[off[i]:off[i + 1]] for i in range(6))
    half = np.ones((REST_WIDTH, 1), np.float32)
    half[S_ZA:S_BU] = 0.5
    half[S_BZ:] = 0.5
    return jnp.concatenate(
        [_deinterleaved(qa, A_HEADS), _deinterleaved(qc, 2 * C_HEADS),
         _deinterleaved(ka, A_KV_HEADS), _deinterleaved(kc, 2 * C_HEADS), va, vc,
         w[off[6]:] * half], axis=0).astype(BF16)


def _rope_tables(t, dim):
    quarter = dim // 4
    tok = jnp.arange(t)
    inv = jnp.power(ROPE_THETA, -jnp.arange(quarter, dtype=F32) / quarter)
    ang = jnp.concatenate([(tok // GRID_W)[:, None].astype(F32) * inv,
                           (tok % GRID_W)[:, None].astype(F32) * inv], axis=-1)
    return jnp.cos(ang).T, jnp.sin(ang).T


def kernel(x, c, ctx, c_ctx, w_mod, b_mod, g_norm, w_in, gq_a, gk_a, gq_c, gk_c, g_sgu, w_sp,
           b_sp, lam_c, g_subln, w_pa, w_pb, w_pc, w_out):
    depth = w_in.shape[0]
    b, t, _ = x.shape
    n_ctx = ctx.shape[1]

    cs = jnp.zeros((SUBLANES, D_MODEL), F32).at[:b].set(c).at[b].set(c_ctx)
    mod = _modulation(cs, w_mod, b_mod)

    cosa, sina = _rope_tables(t, A_HEAD_DIM)
    cosc, sinc = _rope_tables(t, C_QK_DIM)
    one = lambda n: jnp.ones((n, n_ctx), F32)
    zero = lambda n: jnp.zeros((n, n_ctx), F32)
    eye2 = jnp.eye(SGU_BD // B_CHUNK, dtype=F32)

    for l in range(depth):
        lam_init = 0.8 - 0.6 * math.exp(-0.3 * l)
        rowv = lambda v: v.reshape(-1, 1, D_MODEL)
        shift, scale, gate = (rowv(mod[l, :b, i * D_MODEL:(i + 1) * D_MODEL]) for i in range(3))
        shift_c, scale_c, gate_c = (
            rowv(jnp.broadcast_to(mod[l, b, i * D_MODEL:(i + 1) * D_MODEL], (b, D_MODEL)))
            for i in range(3))
        colv = lambda g: _deinterleaved(g, 1).reshape(-1, 1)
        proj = functools.partial(
            _project, g_norm=g_norm[l].reshape(1, D_MODEL), w_t=_projection_rows(w_in[l]),
            gqa=colv(gq_a[l] * (A_HEAD_DIM ** -0.5 * LOG2E)), gka=colv(gk_a[l]),
            gqc=colv(gq_c[l] * (C_QK_DIM ** -0.5 * LOG2E)), gkc=colv(gk_c[l]))

        qa, qc, ka, kc, va, vc, rest = proj(
            x, shift, scale, cosa=cosa, sina=sina, cosc=cosc, sinc=sinc, kv_len=t + n_ctx)
        qa_x, qc_x, ka, kc, va, vc, rest_x = proj(
            ctx, shift_c, scale_c, cosa=one(A_HEAD_DIM // 2), sina=zero(A_HEAD_DIM // 2),
            cosc=one(C_QK_DIM // 2), sinc=zero(C_QK_DIM // 2),
            kv_len=t + n_ctx, kv_start=t, kv_into=(ka, kc, va, vc))
        va, vc = va.reshape(b, A_KV_HEADS, -1, t + n_ctx), vc.reshape(b, C_HEADS, -1, t + n_ctx)

        attend = functools.partial(
            _attend, (gq_a[l], gk_a[l]), (gq_c[l], gk_c[l]), lam_c[l],
            g_subln[l].reshape(C_V_DIM, 1), lam_init=lam_init)
        oa, oc = attend(qa, qc, ka, kc, va, vc)

        wsp_bd = jnp.einsum("ij,gqp->giqjp", eye2, jnp.swapaxes(w_sp[l], 1, 2)).reshape(
            B_GROUPS, SGU_BD, SGU_BD).astype(BF16)
        bsp = jnp.tile(b_sp[l], (1, SGU_BD // B_CHUNK)).reshape(B_GROUPS, 1, SGU_BD)
        merge = functools.partial(
            _merge, gsgu=g_sgu[l].reshape(B_GROUPS, B_GROUP_DIM, 1), wsp_bd=wsp_bd, bsp=bsp,
            wpa_t=w_pa[l].T.astype(BF16), wpb_t=w_pb[l].T.astype(BF16),
            wpc_t=w_pc[l].T.astype(BF16), wout=(0.5 * w_out[l]).astype(BF16))
        x_new = merge(x, gate, oa, oc, rest)
        if l < depth - 1:
            oa_x, oc_x = attend(qa_x, qc_x, ka[:, :, t:], kc[:, :, t:], va[..., t:], vc[..., t:])
            ctx = merge(ctx, gate_c, oa_x, oc_x, rest_x)
        x = x_new
    return x
```

```python
import functools
import math

import numpy as np
import jax
import jax.numpy as jnp
from jax import lax
from jax.experimental import pallas as pl
from jax.experimental.pallas import tpu as pltpu

F32 = jnp.float32
BF16 = jnp.bfloat16

D_MODEL = 1024
GRID_W = 64
ROPE_THETA = 10000.0
EPS = 1e-6
N_BRANCH = 3

A_HEAD_DIM = 64
A_WIDTH = D_MODEL // 2
A_HEADS = A_WIDTH // A_HEAD_DIM
A_KV_HEADS = A_HEADS // 4
A_GROUP = A_HEADS // A_KV_HEADS
A_KV_WIDTH = A_KV_HEADS * A_HEAD_DIM

B_WIDTH = D_MODEL // 4
B_GROUP_DIM = 64
B_GROUPS = B_WIDTH // B_GROUP_DIM
B_CHUNK = 128

C_WIDTH = D_MODEL // 4
C_QK_DIM = 32
C_V_DIM = 2 * C_QK_DIM
C_HEADS = C_WIDTH // C_V_DIM
C_QK_WIDTH = C_HEADS * 2 * C_QK_DIM

R_QA, R_QC, R_KA, R_KC, R_VA, R_VC, R_REST = (int(o) for o in np.cumsum(
    [0, A_WIDTH, C_QK_WIDTH, A_KV_WIDTH, C_QK_WIDTH, A_KV_WIDTH, C_WIDTH]))
S_ZA, S_ZC, S_BU, S_BV, S_BZ, S_GATES, REST_WIDTH = (int(o) for o in np.cumsum(
    [0, A_WIDTH, C_WIDTH, B_WIDTH, B_WIDTH, B_WIDTH, N_BRANCH * D_MODEL]))
PROJ_WIDTH = R_REST + REST_WIDTH

SUBLANES = 8
MXU_DIM = 256
VMEM_LIMIT = 56 * 1024 * 1024

ACC_ROWS = C_V_DIM + SUBLANES
KEY_CHUNK = MXU_DIM
PIPELINE_DEPTH = 2
SGU_BD = 2 * B_CHUNK
MOD_COLS = 1024
REST_SLOTS = 3
TOKEN_TILES = (512, 256, 128)
QUERY_TILES = (2 * MXU_DIM, MXU_DIM, 128)
KEY_TILES = (13 * KEY_CHUNK, 5 * KEY_CHUNK, 4 * KEY_CHUNK, 2 * KEY_CHUNK, KEY_CHUNK)
FALLBACK_QUERY_TILES = (MXU_DIM, 128)
FALLBACK_KEY_TILES = KEY_TILES[1:]
LOG2E = math.log2(math.e)

MIN_DENOM = 2.0 ** -80
STAB_MARGIN = 40.0


def _pick(n, candidates):
    for c in candidates:
        if n % c == 0:
            return c
    raise ValueError(f"no tile for {n} in {candidates}")


def _sigmoid(x):
    return 0.5 * jnp.tanh(0.5 * x) + 0.5


def _silu(x):
    return x * _sigmoid(x)


def _silu_of_twice(h):
    return h * (1.0 + jnp.tanh(h))


def _twice_sigmoid_of_twice(h):
    return 1.0 + jnp.tanh(h)


def _mod_kernel(c_ref, w_ref, b_ref, o_ref):
    s = _silu(c_ref[...])
    o_ref[0] = jnp.dot(s, w_ref[0], preferred_element_type=F32,
                       precision=lax.Precision.HIGHEST) + b_ref[0]


def _modulation(cs, w_mod, b_mod):
    depth = w_mod.shape[0]
    rows = cs.shape[0]
    return pl.pallas_call(
        _mod_kernel,
        grid=(depth, 3 * D_MODEL // MOD_COLS),
        in_specs=[pl.BlockSpec((rows, D_MODEL), lambda l, j: (0, 0)),
                  pl.BlockSpec((1, D_MODEL, MOD_COLS), lambda l, j: (l, 0, j)),
                  pl.BlockSpec((1, 1, MOD_COLS), lambda l, j: (l, 0, j))],
        out_specs=pl.BlockSpec((1, rows, MOD_COLS), lambda l, j: (l, 0, j)),
        out_shape=jax.ShapeDtypeStruct((depth, rows, 3 * D_MODEL), F32),
        compiler_params=pltpu.CompilerParams(
            dimension_semantics=("parallel", "parallel"), vmem_limit_bytes=VMEM_LIMIT),
        name="adaln_mod",
    )(cs, w_mod, b_mod.reshape(depth, 1, 3 * D_MODEL))


def _norm_rope(p, g, cos, sin):
    half = p.shape[0] // 2
    ms = jnp.mean(p * p, axis=0, keepdims=True)
    y = p * lax.rsqrt(ms + EPS) * g
    x0, x1 = y[:half], y[half:]
    return jnp.concatenate([x0 * cos - x1 * sin, x0 * sin + x1 * cos], axis=0)


def _proj_kernel(*refs, n_tiles):
    i = pl.program_id(1)

    @pl.when(i < n_tiles)
    def _tile():
        _proj_tile(*refs)

    @pl.when(i >= n_tiles)
    def _tail():
        for kv_ref in refs[-6:-2]:
            kv_ref[...] = jnp.zeros_like(kv_ref)


def _proj_tile(x_ref, shift_ref, scale_ref, gn_ref, w_ref, gqa_ref, gka_ref, gqc_ref, gkc_ref,
               cosa_ref, sina_ref, cosc_ref, sinc_ref, *refs):
    qa_ref, qc_ref, ka_ref, kc_ref, va_ref, vc_ref, rest_ref, hn_ref = refs[-8:]
    x = x_ref[0]
    ms = jnp.mean(x * x, axis=1, keepdims=True)
    y = x * lax.rsqrt(ms + EPS) * gn_ref[...]
    hn_ref[...] = (y * (1.0 + scale_ref[0]) + shift_ref[0]).T.astype(BF16)

    def proj(r0, r1):
        return jnp.dot(w_ref[r0:r1, :], hn_ref[...], preferred_element_type=F32)

    cosa, sina, cosc, sinc = cosa_ref[...], sina_ref[...], cosc_ref[...], sinc_ref[...]

    pqa = proj(R_QA, R_QC)
    for h in range(A_HEADS):
        r = h * A_HEAD_DIM
        qa_ref[0, r:r + A_HEAD_DIM, :] = _norm_rope(
            pqa[r:r + A_HEAD_DIM], gqa_ref[...], cosa, sina).astype(BF16)
    pqc = proj(R_QC, R_KA)
    for u in range(2 * C_HEADS):
        r = u * C_QK_DIM
        qc_ref[0, r:r + C_QK_DIM, :] = _norm_rope(
            pqc[r:r + C_QK_DIM], gqc_ref[...], cosc, sinc).astype(BF16)
    pka = proj(R_KA, R_KC)
    for h in range(A_KV_HEADS):
        ka_ref[0, h] = _norm_rope(pka[h * A_HEAD_DIM:(h + 1) * A_HEAD_DIM], gka_ref[...],
                                  cosa, sina).T.astype(BF16)
    pkc = proj(R_KC, R_VA)
    for u in range(2 * C_HEADS):
        kc_ref[0, u] = _norm_rope(pkc[u * C_QK_DIM:(u + 1) * C_QK_DIM], gkc_ref[...],
                                  cosc, sinc).T.astype(BF16)
    va_ref[0] = proj(R_VA, R_VC).astype(BF16)
    vc_ref[0] = proj(R_VC, R_REST).astype(BF16)
    for r in range(R_REST, PROJ_WIDTH, 512):
        rest_ref[0, r - R_REST:r - R_REST + 512, :] = proj(r, r + 512).astype(BF16)


def _project(x, shift, scale, g_norm, w_t, gqa, gka, gqc, gkc, cosa, sina, cosc, sinc,
             kv_len, kv_start=0, kv_into=()):
    b, t, _ = x.shape
    tm = _pick(t, TOKEN_TILES)
    assert kv_start % tm == 0
    blk0, n_tiles = kv_start // tm, t // tm
    n_tail = pl.cdiv(kv_len - kv_start - t, tm)
    tile = lambda i: jnp.minimum(i, n_tiles - 1)
    col = lambda n: pl.BlockSpec((n, 1), lambda bi, i: (0, 0))
    tok = lambda n: pl.BlockSpec((n, tm), lambda bi, i: (0, tile(i)))
    outT = lambda n: pl.BlockSpec((1, n, tm), lambda bi, i: (bi, 0, tile(i)))
    kv_tok = lambda n, d: pl.BlockSpec((1, n, tm, d), lambda bi, i: (bi, 0, i + blk0, 0))
    kv_chan = lambda n: pl.BlockSpec((1, n, tm), lambda bi, i: (bi, 0, i + blk0))
    n_in = 13
    return pl.pallas_call(
        functools.partial(_proj_kernel, n_tiles=n_tiles),
        grid=(b, n_tiles + n_tail),
        in_specs=[pl.BlockSpec((1, tm, D_MODEL), lambda bi, i: (bi, tile(i), 0)),
                  pl.BlockSpec((1, 1, D_MODEL), lambda bi, i: (bi, 0, 0)),
                  pl.BlockSpec((1, 1, D_MODEL), lambda bi, i: (bi, 0, 0)),
                  pl.BlockSpec((1, D_MODEL), lambda bi, i: (0, 0)),
                  pl.BlockSpec((PROJ_WIDTH, D_MODEL), lambda bi, i: (0, 0),
                               pipeline_mode=pl.Buffered(1)),
                  col(A_HEAD_DIM), col(A_HEAD_DIM), col(C_QK_DIM), col(C_QK_DIM),
                  tok(A_HEAD_DIM // 2), tok(A_HEAD_DIM // 2), tok(C_QK_DIM // 2), tok(C_QK_DIM // 2)]
                 + [pl.BlockSpec(memory_space=pl.ANY)] * len(kv_into),
        out_specs=[outT(A_WIDTH), outT(C_QK_WIDTH),
                   kv_tok(A_KV_HEADS, A_HEAD_DIM), kv_tok(2 * C_HEADS, C_QK_DIM),
                   kv_chan(A_KV_WIDTH), kv_chan(C_WIDTH), outT(REST_WIDTH)],
        out_shape=[jax.ShapeDtypeStruct((b, A_WIDTH, t), BF16),
                   jax.ShapeDtypeStruct((b, C_QK_WIDTH, t), BF16),
                   jax.ShapeDtypeStruct((b, A_KV_HEADS, kv_len, A_HEAD_DIM), BF16),
                   jax.ShapeDtypeStruct((b, 2 * C_HEADS, kv_len, C_QK_DIM), BF16),
                   jax.ShapeDtypeStruct((b, A_KV_WIDTH, kv_len), BF16),
                   jax.ShapeDtypeStruct((b, C_WIDTH, kv_len), BF16),
                   jax.ShapeDtypeStruct((b, REST_WIDTH, t), BF16)],
        input_output_aliases={n_in + j: 2 + j for j in range(len(kv_into))},
        scratch_shapes=[pltpu.VMEM((D_MODEL, tm), BF16)],
        compiler_params=pltpu.CompilerParams(
            dimension_semantics=("parallel", "arbitrary"), vmem_limit_bytes=VMEM_LIMIT),
        name="in_proj",
    )(x, shift, scale, g_norm, w_t, gqa, gka, gqc, gkc, cosa, sina, cosc, sinc, *kv_into)


def _attn_kernel(*refs, lam_init, online):
    if online:
        (lam_ref, gsub_ref, qa_ref, qc_ref, ka_ref, kc_ref, va_ref, vc_ref,
         oa_ref, oc_ref, acc_a, acc_c, m_a, m_c) = refs
    else:
        (stab_ref, lam_ref, gsub_ref, qa_ref, qc_ref, ka_ref, kc_ref, va_ref, vc_ref,
         oa_ref, oc_ref, lmin_ref, acc_a, acc_c) = refs
        m_a = m_c = None
    kv = pl.program_id(2)

    @pl.when(kv == 0)
    def _init():
        acc_a[...] = jnp.zeros_like(acc_a)
        acc_c[...] = jnp.zeros_like(acc_c)
        if online:
            m_a[...] = jnp.full_like(m_a, -1e30)
            m_c[...] = jnp.full_like(m_c, -1e30)

    def probs(s, shift):
        pf = jnp.exp2(s - shift)
        return pf.astype(BF16), pf.reshape(-1, SUBLANES, s.shape[1]).sum(axis=0)

    def weighted(v, p, lsum):
        return jnp.concatenate([jnp.dot(v, p, preferred_element_type=F32), lsum], axis=0)

    tk = ka_ref.shape[2]
    units = []
    for hd in range(A_HEADS):
        q = qa_ref[0, hd * A_HEAD_DIM:(hd + 1) * A_HEAD_DIM, :]
        units.append((acc_a, hd, ka_ref, hd // A_GROUP, q, hd // A_GROUP, va_ref))
    for u in range(2 * C_HEADS):
        q = qc_ref[0, u * C_QK_DIM:(u + 1) * C_QK_DIM, :]
        units.append((acc_c, u, kc_ref, u, q, u // 2, vc_ref))

    if online:
        for acc_ref, u, k_ref, kh, q, vh, v_ref in units:
            m_ref = m_a if acc_ref is acc_a else m_c
            s = jnp.dot(k_ref[0, kh], q, preferred_element_type=F32)
            m_prev = m_ref[u:u + 1, :]
            m_new = jnp.maximum(m_prev, jnp.max(s, axis=0, keepdims=True))
            m_ref[u:u + 1, :] = m_new
            acc_ref[u] = (acc_ref[u] * jnp.exp2(m_prev - m_new)
                          + weighted(v_ref[0, vh], *probs(s, m_new)))
    else:
        n_chunks = tk // KEY_CHUNK
        pending, partial = [], {}

        def retire():
            ui, ci, p, lsum = pending.pop(0)
            acc_ref, u, _, _, _, vh, v_ref = units[ui]
            pv = weighted(v_ref[0, vh, :, ci * KEY_CHUNK:(ci + 1) * KEY_CHUNK], p, lsum)
            partial[ui] = pv if ci == 0 else partial[ui] + pv
            if ci == n_chunks - 1:
                acc_ref[u] += partial.pop(ui)

        for ui, (acc_ref, u, k_ref, kh, q, vh, v_ref) in enumerate(units):
            stab = stab_ref[0] if acc_ref is acc_a else stab_ref[1]
            for ci in range(n_chunks):
                s = jnp.dot(k_ref[0, kh, ci * KEY_CHUNK:(ci + 1) * KEY_CHUNK, :], q,
                            preferred_element_type=F32)
                pending.append((ui, ci) + probs(s, stab))
                if len(pending) > PIPELINE_DEPTH:
                    retire()
        while pending:
            retire()

    @pl.when(kv == pl.num_programs(2) - 1)
    def _finish():
        def normalised(acc, d):
            l = jnp.sum(acc[d:], axis=0, keepdims=True)
            return acc[:d] / l, l

        lmin = None
        for hd in range(A_HEADS):
            o, l = normalised(acc_a[hd], A_HEAD_DIM)
            lmin = l if lmin is None else jnp.minimum(lmin, l)
            oa_ref[0, hd * A_HEAD_DIM:(hd + 1) * A_HEAD_DIM, :] = o.astype(BF16)
        lam_c = lam_ref[...]
        lam = (jnp.exp(jnp.sum(lam_c[0:1] * lam_c[1:2], axis=1, keepdims=True))
               - jnp.exp(jnp.sum(lam_c[2:3] * lam_c[3:4], axis=1, keepdims=True)) + lam_init)
        for h in range(C_HEADS):
            (o1, l1), (o2, l2) = (normalised(acc_c[2 * h + j], C_V_DIM) for j in range(2))
            lmin = jnp.minimum(lmin, jnp.minimum(l1, l2))
            o = o1 - lam * o2
            ms = jnp.mean(o * o, axis=0, keepdims=True)
            o = o * lax.rsqrt(ms + EPS) * gsub_ref[...] * (1.0 - lam_init)
            oc_ref[0, h * C_V_DIM:(h + 1) * C_V_DIM, :] = o.astype(BF16)
        if not online:
            lmin_ref[0, 0] = jnp.broadcast_to(lmin, lmin_ref.shape[2:])


def _attention(stab, lam_c, g_subln, qa, qc, ka, kc, va, vc, lam_init, online):
    b, _, t = qa.shape
    tkv = ka.shape[2]
    if online:
        tq, tk = _pick(t, FALLBACK_QUERY_TILES), _pick(tkv, FALLBACK_KEY_TILES)
    else:
        tq, tk = _pick(t, QUERY_TILES), _pick(tkv, KEY_TILES)
    nq = t // tq
    in_specs = [pl.BlockSpec((4, C_QK_DIM), lambda bi, i, j: (0, 0)),
                pl.BlockSpec((C_V_DIM, 1), lambda bi, i, j: (0, 0)),
                pl.BlockSpec((1, A_WIDTH, tq), lambda bi, i, j: (bi, 0, i)),
                pl.BlockSpec((1, C_QK_WIDTH, tq), lambda bi, i, j: (bi, 0, i)),
                pl.BlockSpec((1, A_KV_HEADS, tk, A_HEAD_DIM), lambda bi, i, j: (bi, 0, j, 0)),
                pl.BlockSpec((1, 2 * C_HEADS, tk, C_QK_DIM), lambda bi, i, j: (bi, 0, j, 0)),
                pl.BlockSpec((1, A_KV_HEADS, A_HEAD_DIM, tk), lambda bi, i, j: (bi, 0, 0, j)),
                pl.BlockSpec((1, C_HEADS, C_V_DIM, tk), lambda bi, i, j: (bi, 0, 0, j))]
    out_specs = [pl.BlockSpec((1, A_WIDTH, tq), lambda bi, i, j: (bi, 0, i)),
                 pl.BlockSpec((1, C_WIDTH, tq), lambda bi, i, j: (bi, 0, i))]
    out_shape = [jax.ShapeDtypeStruct((b, A_WIDTH, t), BF16),
                 jax.ShapeDtypeStruct((b, C_WIDTH, t), BF16)]
    scratch = [pltpu.VMEM((A_HEADS, ACC_ROWS, tq), F32),
               pltpu.VMEM((2 * C_HEADS, ACC_ROWS, tq), F32)]
    args = (lam_c, g_subln, qa, qc, ka, kc, va, vc)
    if online:
        scratch += [pltpu.VMEM((A_HEADS, tq), F32), pltpu.VMEM((2 * C_HEADS, tq), F32)]
    else:
        in_specs = [pl.BlockSpec(memory_space=pltpu.SMEM)] + in_specs
        out_specs.append(pl.BlockSpec((1, 1, SUBLANES, tq), lambda bi, i, j: (bi, i, 0, 0)))
        out_shape.append(jax.ShapeDtypeStruct((b, nq, SUBLANES, tq), F32))
        args = (stab,) + args
    return pl.pallas_call(
        functools.partial(_attn_kernel, lam_init=lam_init, online=online),
        grid=(b, nq, tkv // tk),
        in_specs=in_specs, out_specs=out_specs, out_shape=out_shape, scratch_shapes=scratch,
        compiler_params=pltpu.CompilerParams(
            dimension_semantics=("parallel", "parallel", "arbitrary"),
            vmem_limit_bytes=VMEM_LIMIT),
        name="attention_online" if online else "attention_fixed",
    )(*args)


def _attend(gains_a, gains_c, lam_c, g_subln, qa, qc, ka, kc, va, vc, lam_init):
    bound = lambda gains, d: (1.02 * LOG2E * d ** 0.5
                              * jnp.max(jnp.abs(gains[0])) * jnp.max(jnp.abs(gains[1])))
    stab = jnp.stack([bound(gains_a, A_HEAD_DIM), bound(gains_c, C_QK_DIM)]) - STAB_MARGIN
    args = (lam_c, g_subln, qa, qc, ka, kc, va, vc, lam_init)
    oa, oc, lmin = _attention(stab.astype(F32), *args, online=False)
    return lax.cond(jnp.all(lmin >= MIN_DENOM), lambda: (oa, oc),
                    lambda: tuple(_attention(None, *args, online=True)))


def _merge_kernel(x_ref, gate_ref, oa_ref, oc_ref, rest_hbm, gsgu_ref, wsp_ref, bsp_ref,
                  wpa_ref, wpb_ref, wpc_ref, wout_ref, o_ref, mt_ref, rest_buf, rest_sem):
    tm = x_ref.shape[1]
    n_tiles = pl.num_programs(1)
    step = pl.program_id(0) * n_tiles + pl.program_id(1)
    n_steps = pl.num_programs(0) * n_tiles

    def rest_copy(s):
        slot = s % REST_SLOTS
        tok = pl.multiple_of((s % n_tiles) * tm, tm)
        return pltpu.make_async_copy(rest_hbm.at[s // n_tiles, :, pl.ds(tok, tm)],
                                     rest_buf.at[slot], rest_sem.at[slot])

    @pl.when(step == 0)
    def _prime():
        for s in range(REST_SLOTS - 1):
            @pl.when(s < n_steps)
            def _():
                rest_copy(s).start()

    @pl.when(step + REST_SLOTS - 1 < n_steps)
    def _prefetch():
        rest_copy(step + REST_SLOTS - 1).start()

    rest_copy(step).wait()
    rest_ref = rest_buf.at[step % REST_SLOTS]
    rest = lambda r0, n: rest_ref[r0:r0 + n, :].astype(F32)

    mixed_groups = []
    for g in range(B_GROUPS):
        v = rest(S_BV + g * B_GROUP_DIM, B_GROUP_DIM)
        ms = jnp.mean(v * v, axis=0, keepdims=True)
        vn = (v * lax.rsqrt(ms + EPS) * gsgu_ref[g]).astype(BF16)
        cols = [jnp.dot(vn[:, c:c + SGU_BD], wsp_ref[g], preferred_element_type=F32) + bsp_ref[g]
                for c in range(0, tm, SGU_BD)]
        mixed_groups.append(cols[0] if len(cols) == 1 else jnp.concatenate(cols, axis=1))
    ha = (oa_ref[0].astype(F32) * _silu_of_twice(rest(S_ZA, A_WIDTH))).astype(BF16)
    ya = jnp.dot(wpa_ref[...], ha, preferred_element_type=F32)
    hc = (oc_ref[0].astype(F32) * _silu_of_twice(rest(S_ZC, C_WIDTH))).astype(BF16)
    yc = jnp.dot(wpc_ref[...], hc, preferred_element_type=F32)
    mixed = jnp.concatenate(mixed_groups, axis=0)
    hb = (rest(S_BU, B_WIDTH) * mixed * _silu_of_twice(rest(S_BZ, B_WIDTH))).astype(BF16)
    yb = jnp.dot(wpb_ref[...], hb, preferred_element_type=F32)

    m = (_twice_sigmoid_of_twice(rest(S_GATES, D_MODEL)) * ya
         + _twice_sigmoid_of_twice(rest(S_GATES + D_MODEL, D_MODEL)) * yb
         + _twice_sigmoid_of_twice(rest(S_GATES + 2 * D_MODEL, D_MODEL)) * yc)
    mt_ref[...] = m.T.astype(BF16)
    out = jnp.dot(mt_ref[...], wout_ref[...], preferred_element_type=F32)
    o_ref[0] = x_ref[0] + gate_ref[0] * out


def _merge(x, gate, oa, oc, rest, gsgu, wsp_bd, bsp, wpa_t, wpb_t, wpc_t, wout):
    b, t, _ = x.shape
    tm = _pick(t, TOKEN_TILES[:2])
    tokT = lambda n: pl.BlockSpec((1, n, tm), lambda bi, i: (bi, 0, i))
    full = lambda shape: pl.BlockSpec(shape, lambda bi, i: (0,) * len(shape))
    return pl.pallas_call(
        _merge_kernel,
        grid=(b, t // tm),
        in_specs=[pl.BlockSpec((1, tm, D_MODEL), lambda bi, i: (bi, i, 0)),
                  pl.BlockSpec((1, 1, D_MODEL), lambda bi, i: (bi, 0, 0)),
                  tokT(A_WIDTH), tokT(C_WIDTH), pl.BlockSpec(memory_space=pl.ANY),
                  full((B_GROUPS, B_GROUP_DIM, 1)), full((B_GROUPS, SGU_BD, SGU_BD)),
                  full((B_GROUPS, 1, SGU_BD)),
                  full((D_MODEL, A_WIDTH)), full((D_MODEL, B_WIDTH)), full((D_MODEL, C_WIDTH)),
                  full((D_MODEL, D_MODEL))],
        out_specs=pl.BlockSpec((1, tm, D_MODEL), lambda bi, i: (bi, i, 0)),
        out_shape=jax.ShapeDtypeStruct((b, t, D_MODEL), F32),
        scratch_shapes=[pltpu.VMEM((tm, D_MODEL), BF16),
                        pltpu.VMEM((REST_SLOTS, REST_WIDTH, tm), BF16),
                        pltpu.SemaphoreType.DMA((REST_SLOTS,))],
        compiler_params=pltpu.CompilerParams(
            dimension_semantics=("arbitrary", "arbitrary"), vmem_limit_bytes=VMEM_LIMIT),
        name="merge_out",
    )(x, gate, oa, oc, rest, gsgu, wsp_bd, bsp, wpa_t, wpb_t, wpc_t, wout)


def _deinterleaved(w, n_heads):
    dim = w.shape[0] // n_heads
    w = w.reshape(n_heads, dim // 2, 2, *w.shape[1:])
    return jnp.swapaxes(w, 1, 2).reshape(n_heads * dim, *w.shape[3:])


def _projection_rows(w_in):
    off = np.cumsum([0, A_WIDTH, C_QK_WIDTH, A_KV_WIDTH, A_KV_WIDTH, C_QK_WIDTH, C_WIDTH])
    w = w_in.T
    qa, qc, ka, va, kc, vc = (w[off[i]:off[i + 1]] for i in range(6))
    half = np.ones((REST_WIDTH, 1), np.float32)
    half[S_ZA:S_BU] = 0.5
    half[S_BZ:] = 0.5
    return jnp.concatenate(
        [_deinterleaved(qa, A_HEADS), _deinterleaved(qc, 2 * C_HEADS),
         _deinterleaved(ka, A_KV_HEADS), _deinterleaved(kc, 2 * C_HEADS), va, vc,
         w[off[6]:] * half], axis=0).astype(BF16)


def _rope_tables(t, dim):
    quarter = dim // 4
    tok = jnp.arange(t)
    inv = jnp.power(ROPE_THETA, -jnp.arange(quarter, dtype=F32) / quarter)
    ang = jnp.concatenate([(tok // GRID_W)[:, None].astype(F32) * inv,
                           (tok % GRID_W)[:, None].astype(F32) * inv], axis=-1)
    return jnp.cos(ang).T, jnp.sin(ang).T


def kernel(x, c, ctx, c_ctx, w_mod, b_mod, g_norm, w_in, gq_a, gk_a, gq_c, gk_c, g_sgu, w_sp,
           b_sp, lam_c, g_subln, w_pa, w_pb, w_pc, w_out):
    depth = w_in.shape[0]
    b, t, _ = x.shape
    n_ctx = ctx.shape[1]

    cs = jnp.zeros((SUBLANES, D_MODEL), F32).at[:b].set(c).at[b].set(c_ctx)
    mod = _modulation(cs, w_mod, b_mod)

    cosa, sina = _rope_tables(t, A_HEAD_DIM)
    cosc, sinc = _rope_tables(t, C_QK_DIM)
    one = lambda n: jnp.ones((n, n_ctx), F32)
    zero = lambda n: jnp.zeros((n, n_ctx), F32)
    eye2 = jnp.eye(SGU_BD // B_CHUNK, dtype=F32)

    for l in range(depth):
        lam_init = 0.8 - 0.6 * math.exp(-0.3 * l)
        rowv = lambda v: v.reshape(-1, 1, D_MODEL)
        shift, scale, gate = (rowv(mod[l, :b, i * D_MODEL:(i + 1) * D_MODEL]) for i in range(3))
        shift_c, scale_c, gate_c = (
            rowv(jnp.broadcast_to(mod[l, b, i * D_MODEL:(i + 1) * D_MODEL], (b, D_MODEL)))
            for i in range(3))
        colv = lambda g: _deinterleaved(g, 1).reshape(-1, 1)
        proj = functools.partial(
            _project, g_norm=g_norm[l].reshape(1, D_MODEL), w_t=_projection_rows(w_in[l]),
            gqa=colv(gq_a[l] * (A_HEAD_DIM ** -0.5 * LOG2E)), gka=colv(gk_a[l]),
            gqc=colv(gq_c[l] * (C_QK_DIM ** -0.5 * LOG2E)), gkc=colv(gk_c[l]))

        qa, qc, ka, kc, va, vc, rest = proj(
            x, shift, scale, cosa=cosa, sina=sina, cosc=cosc, sinc=sinc, kv_len=t + n_ctx)
        qa_x, qc_x, ka, kc, va, vc, rest_x = proj(
            ctx, shift_c, scale_c, cosa=one(A_HEAD_DIM // 2), sina=zero(A_HEAD_DIM // 2),
            cosc=one(C_QK_DIM // 2), sinc=zero(C_QK_DIM // 2),
            kv_len=t + n_ctx, kv_start=t, kv_into=(ka, kc, va, vc))
        va, vc = va.reshape(b, A_KV_HEADS, -1, t + n_ctx), vc.reshape(b, C_HEADS, -1, t + n_ctx)

        attend = functools.partial(
            _attend, (gq_a[l], gk_a[l]), (gq_c[l], gk_c[l]), lam_c[l],
            g_subln[l].reshape(C_V_DIM, 1), lam_init=lam_init)
        oa, oc = attend(qa, qc, ka, kc, va, vc)

        wsp_bd = jnp.einsum("ij,gqp->giqjp", eye2, jnp.swapaxes(w_sp[l], 1, 2)).reshape(
            B_GROUPS, SGU_BD, SGU_BD).astype(BF16)
        bsp = jnp.tile(b_sp[l], (1, SGU_BD // B_CHUNK)).reshape(B_GROUPS, 1, SGU_BD)
        merge = functools.partial(
            _merge, gsgu=g_sgu[l].reshape(B_GROUPS, B_GROUP_DIM, 1), wsp_bd=wsp_bd, bsp=bsp,
            wpa_t=w_pa[l].T.astype(BF16), wpb_t=w_pb[l].T.astype(BF16),
            wpc_t=w_pc[l].T.astype(BF16), wout=(0.5 * w_out[l]).astype(BF16))
        x_new = merge(x, gate, oa, oc, rest)
        if l < depth - 1:
            oa_x, oc_x = attend(qa_x, qc_x, ka[:, :, t:], kc[:, :, t:], va[..., t:], vc[..., t:])
            ctx = merge(ctx, gate_c, oa_x, oc_x, rest_x)
        x = x_new
    return x
```

```python
import functools
import math

import numpy as np
import jax
import jax.numpy as jnp
from jax import lax
from jax.experimental import pallas as pl
from jax.experimental.pallas import tpu as pltpu

F32 = jnp.float32
BF16 = jnp.bfloat16

D_MODEL = 1024
GRID_W = 64
ROPE_THETA = 10000.0
EPS = 1e-6
N_BRANCH = 3

A_HEAD_DIM = 64
A_WIDTH = D_MODEL // 2
A_HEADS = A_WIDTH // A_HEAD_DIM
A_KV_HEADS = A_HEADS // 4
A_GROUP = A_HEADS // A_KV_HEADS
A_KV_WIDTH = A_KV_HEADS * A_HEAD_DIM

B_WIDTH = D_MODEL // 4
B_GROUP_DIM = 64
B_GROUPS = B_WIDTH // B_GROUP_DIM
B_CHUNK = 128

C_WIDTH = D_MODEL // 4
C_QK_DIM = 32
C_V_DIM = 2 * C_QK_DIM
C_HEADS = C_WIDTH // C_V_DIM
C_QK_WIDTH = C_HEADS * 2 * C_QK_DIM

R_QA, R_QC, R_KA, R_KC, R_VA, R_VC, R_REST = (int(o) for o in np.cumsum(
    [0, A_WIDTH, C_QK_WIDTH, A_KV_WIDTH, C_QK_WIDTH, A_KV_WIDTH, C_WIDTH]))
S_ZA, S_ZC, S_BU, S_BV, S_BZ, S_GATES, REST_WIDTH = (int(o) for o in np.cumsum(
    [0, A_WIDTH, C_WIDTH, B_WIDTH, B_WIDTH, B_WIDTH, N_BRANCH * D_MODEL]))
PROJ_WIDTH = R_REST + REST_WIDTH

SUBLANES = 8
MXU_DIM = 256
VMEM_LIMIT = 56 * 1024 * 1024

ACC_ROWS = C_V_DIM + SUBLANES
KEY_CHUNK = MXU_DIM
PIPELINE_DEPTH = 2
SGU_BD = 2 * B_CHUNK
MOD_COLS = 1024
SUB_TILE = 256
REST_SLOTS = 3
TOKEN_TILES = (512, 256, 128)
QUERY_TILES = (2 * MXU_DIM, MXU_DIM, 128)
KEY_TILES = (13 * KEY_CHUNK, 5 * KEY_CHUNK, 4 * KEY_CHUNK, 2 * KEY_CHUNK, KEY_CHUNK)
FALLBACK_QUERY_TILES = (MXU_DIM, 128)
FALLBACK_KEY_TILES = KEY_TILES[1:]
LOG2E = math.log2(math.e)

MIN_DENOM = 2.0 ** -80
STAB_MARGIN = 40.0


def _pick(n, candidates):
    for c in candidates:
        if n % c == 0:
            return c
    raise ValueError(f"no tile for {n} in {candidates}")


def _sigmoid(x):
    return 0.5 * jnp.tanh(0.5 * x) + 0.5


def _silu(x):
    return x * _sigmoid(x)


def _silu_of_twice(h):
    return h * (1.0 + jnp.tanh(h))


def _twice_sigmoid_of_twice(h):
    return 1.0 + jnp.tanh(h)


def _mod_kernel(c_ref, w_ref, b_ref, o_ref):
    s = _silu(c_ref[...])
    o_ref[0] = jnp.dot(s, w_ref[0], preferred_element_type=F32,
                       precision=lax.Precision.HIGHEST) + b_ref[0]


def _modulation(cs, w_mod, b_mod):
    depth = w_mod.shape[0]
    rows = cs.shape[0]
    return pl.pallas_call(
        _mod_kernel,
        grid=(depth, 3 * D_MODEL // MOD_COLS),
        in_specs=[pl.BlockSpec((rows, D_MODEL), lambda l, j: (0, 0)),
                  pl.BlockSpec((1, D_MODEL, MOD_COLS), lambda l, j: (l, 0, j)),
                  pl.BlockSpec((1, 1, MOD_COLS), lambda l, j: (l, 0, j))],
        out_specs=pl.BlockSpec((1, rows, MOD_COLS), lambda l, j: (l, 0, j)),
        out_shape=jax.ShapeDtypeStruct((depth, rows, 3 * D_MODEL), F32),
        compiler_params=pltpu.CompilerParams(
            dimension_semantics=("parallel", "parallel"), vmem_limit_bytes=VMEM_LIMIT),
        name="adaln_mod",
    )(cs, w_mod, b_mod.reshape(depth, 1, 3 * D_MODEL))


def _norm_rope(p, g, cos, sin):
    half = p.shape[0] // 2
    ms = jnp.mean(p * p, axis=0, keepdims=True)
    y = p * lax.rsqrt(ms + EPS) * g
    x0, x1 = y[:half], y[half:]
    return jnp.concatenate([x0 * cos - x1 * sin, x0 * sin + x1 * cos], axis=0)


def _proj_kernel(*refs, n_tiles):
    i = pl.program_id(1)

    @pl.when(i < n_tiles)
    def _tile():
        _proj_tile(*refs)

    @pl.when(i >= n_tiles)
    def _tail():
        for kv_ref in refs[-6:-2]:
            kv_ref[...] = jnp.zeros_like(kv_ref)


def _proj_tile(x_ref, shift_ref, scale_ref, gn_ref, w_ref, gqa_ref, gka_ref, gqc_ref, gkc_ref,
               cosa_ref, sina_ref, cosc_ref, sinc_ref, *refs):
    qa_ref, qc_ref, ka_ref, kc_ref, va_ref, vc_ref, rest_ref, hn_ref = refs[-8:]
    tm = x_ref.shape[1]
    sub = min(tm, SUB_TILE)
    for c0 in range(0, tm, sub):
        cs = slice(c0, c0 + sub)
        x = x_ref[0, cs, :]
        ms = jnp.mean(x * x, axis=1, keepdims=True)
        y = x * lax.rsqrt(ms + EPS) * gn_ref[...]
        hn_ref[:, cs] = (y * (1.0 + scale_ref[0]) + shift_ref[0]).T.astype(BF16)

        def proj(r0, r1):
            return jnp.dot(w_ref[r0:r1, :], hn_ref[:, cs], preferred_element_type=F32)

        cosa, sina, cosc, sinc = cosa_ref[:, cs], sina_ref[:, cs], cosc_ref[:, cs], sinc_ref[:, cs]

        pqa = proj(R_QA, R_QC)
        for h in range(A_HEADS):
            r = h * A_HEAD_DIM
            qa_ref[0, r:r + A_HEAD_DIM, cs] = _norm_rope(
                pqa[r:r + A_HEAD_DIM], gqa_ref[...], cosa, sina).astype(BF16)
        pqc = proj(R_QC, R_KA)
        for u in range(2 * C_HEADS):
            r = u * C_QK_DIM
            qc_ref[0, r:r + C_QK_DIM, cs] = _norm_rope(
                pqc[r:r + C_QK_DIM], gqc_ref[...], cosc, sinc).astype(BF16)
        pka = proj(R_KA, R_KC)
        for h in range(A_KV_HEADS):
            ka_ref[0, h, cs, :] = _norm_rope(pka[h * A_HEAD_DIM:(h + 1) * A_HEAD_DIM], gka_ref[...],
                                             cosa, sina).T.astype(BF16)
        pkc = proj(R_KC, R_VA)
        for u in range(2 * C_HEADS):
            kc_ref[0, u, cs, :] = _norm_rope(pkc[u * C_QK_DIM:(u + 1) * C_QK_DIM], gkc_ref[...],
                                             cosc, sinc).T.astype(BF16)
        va_ref[0, :, cs] = proj(R_VA, R_VC).astype(BF16)
        vc_ref[0, :, cs] = proj(R_VC, R_REST).astype(BF16)
        for r in range(R_REST, PROJ_WIDTH, 512):
            rest_ref[0, r - R_REST:r - R_REST + 512, cs] = proj(r, r + 512).astype(BF16)


def _project(x, shift, scale, g_norm, w_t, gqa, gka, gqc, gkc, cosa, sina, cosc, sinc,
             kv_len, kv_start=0, kv_into=()):
    b, t, _ = x.shape
    tm = _pick(t, TOKEN_TILES)
    assert kv_start % tm == 0
    blk0, n_tiles = kv_start // tm, t // tm
    n_tail = pl.cdiv(kv_len - kv_start - t, tm)
    tile = lambda i: jnp.minimum(i, n_tiles - 1)
    col = lambda n: pl.BlockSpec((n, 1), lambda bi, i: (0, 0))
    tok = lambda n: pl.BlockSpec((n, tm), lambda bi, i: (0, tile(i)))
    outT = lambda n: pl.BlockSpec((1, n, tm), lambda bi, i: (bi, 0, tile(i)))
    kv_tok = lambda n, d: pl.BlockSpec((1, n, tm, d), lambda bi, i: (bi, 0, i + blk0, 0))
    kv_chan = lambda n: pl.BlockSpec((1, n, tm), lambda bi, i: (bi, 0, i + blk0))
    n_in = 13
    return pl.pallas_call(
        functools.partial(_proj_kernel, n_tiles=n_tiles),
        grid=(b, n_tiles + n_tail),
        in_specs=[pl.BlockSpec((1, tm, D_MODEL), lambda bi, i: (bi, tile(i), 0)),
                  pl.BlockSpec((1, 1, D_MODEL), lambda bi, i: (bi, 0, 0)),
                  pl.BlockSpec((1, 1, D_MODEL), lambda bi, i: (bi, 0, 0)),
                  pl.BlockSpec((1, D_MODEL), lambda bi, i: (0, 0)),
                  pl.BlockSpec((PROJ_WIDTH, D_MODEL), lambda bi, i: (0, 0),
                               pipeline_mode=pl.Buffered(1)),
                  col(A_HEAD_DIM), col(A_HEAD_DIM), col(C_QK_DIM), col(C_QK_DIM),
                  tok(A_HEAD_DIM // 2), tok(A_HEAD_DIM // 2), tok(C_QK_DIM // 2), tok(C_QK_DIM // 2)]
                 + [pl.BlockSpec(memory_space=pl.ANY)] * len(kv_into),
        out_specs=[outT(A_WIDTH), outT(C_QK_WIDTH),
                   kv_tok(A_KV_HEADS, A_HEAD_DIM), kv_tok(2 * C_HEADS, C_QK_DIM),
                   kv_chan(A_KV_WIDTH), kv_chan(C_WIDTH), outT(REST_WIDTH)],
        out_shape=[jax.ShapeDtypeStruct((b, A_WIDTH, t), BF16),
                   jax.ShapeDtypeStruct((b, C_QK_WIDTH, t), BF16),
                   jax.ShapeDtypeStruct((b, A_KV_HEADS, kv_len, A_HEAD_DIM), BF16),
                   jax.ShapeDtypeStruct((b, 2 * C_HEADS, kv_len, C_QK_DIM), BF16),
                   jax.ShapeDtypeStruct((b, A_KV_WIDTH, kv_len), BF16),
                   jax.ShapeDtypeStruct((b, C_WIDTH, kv_len), BF16),
                   jax.ShapeDtypeStruct((b, REST_WIDTH, t), BF16)],
        input_output_aliases={n_in + j: 2 + j for j in range(len(kv_into))},
        scratch_shapes=[pltpu.VMEM((D_MODEL, tm), BF16)],
        compiler_params=pltpu.CompilerParams(
            dimension_semantics=("parallel", "arbitrary"), vmem_limit_bytes=VMEM_LIMIT),
        name="in_proj",
    )(x, shift, scale, g_norm, w_t, gqa, gka, gqc, gkc, cosa, sina, cosc, sinc, *kv_into)


def _attn_kernel(*refs, lam_init, online):
    if online:
        (lam_ref, gsub_ref, qa_ref, qc_ref, ka_ref, kc_ref, va_ref, vc_ref,
         oa_ref, oc_ref, acc_a, acc_c, m_a, m_c) = refs
    else:
        (stab_ref, lam_ref, gsub_ref, qa_ref, qc_ref, ka_ref, kc_ref, va_ref, vc_ref,
         oa_ref, oc_ref, lmin_ref, acc_a, acc_c) = refs
        m_a = m_c = None
    kv = pl.program_id(2)

    @pl.when(kv == 0)
    def _init():
        acc_a[...] = jnp.zeros_like(acc_a)
        acc_c[...] = jnp.zeros_like(acc_c)
        if online:
            m_a[...] = jnp.full_like(m_a, -1e30)
            m_c[...] = jnp.full_like(m_c, -1e30)

    def probs(s, shift):
        pf = jnp.exp2(s - shift)
        return pf.astype(BF16), pf.reshape(-1, SUBLANES, s.shape[1]).sum(axis=0)

    def weighted(v, p, lsum):
        return jnp.concatenate([jnp.dot(v, p, preferred_element_type=F32), lsum], axis=0)

    tk = ka_ref.shape[2]
    units = []
    for hd in range(A_HEADS):
        q = qa_ref[0, hd * A_HEAD_DIM:(hd + 1) * A_HEAD_DIM, :]
        units.append((acc_a, hd, ka_ref, hd // A_GROUP, q, hd // A_GROUP, va_ref))
    for u in range(2 * C_HEADS):
        q = qc_ref[0, u * C_QK_DIM:(u + 1) * C_QK_DIM, :]
        units.append((acc_c, u, kc_ref, u, q, u // 2, vc_ref))

    if online:
        for acc_ref, u, k_ref, kh, q, vh, v_ref in units:
            m_ref = m_a if acc_ref is acc_a else m_c
            s = jnp.dot(k_ref[0, kh], q, preferred_element_type=F32)
            m_prev = m_ref[u:u + 1, :]
            m_new = jnp.maximum(m_prev, jnp.max(s, axis=0, keepdims=True))
            m_ref[u:u + 1, :] = m_new
            acc_ref[u] = (acc_ref[u] * jnp.exp2(m_prev - m_new)
                          + weighted(v_ref[0, vh], *probs(s, m_new)))
    else:
        n_chunks = tk // KEY_CHUNK
        pending, partial = [], {}

        def retire():
            ui, ci, p, lsum = pending.pop(0)
            acc_ref, u, _, _, _, vh, v_ref = units[ui]
            pv = weighted(v_ref[0, vh, :, ci * KEY_CHUNK:(ci + 1) * KEY_CHUNK], p, lsum)
            partial[ui] = pv if ci == 0 else partial[ui] + pv
            if ci == n_chunks - 1:
                acc_ref[u] += partial.pop(ui)

        for ui, (acc_ref, u, k_ref, kh, q, vh, v_ref) in enumerate(units):
            stab = stab_ref[0] if acc_ref is acc_a else stab_ref[1]
            for ci in range(n_chunks):
                s = jnp.dot(k_ref[0, kh, ci * KEY_CHUNK:(ci + 1) * KEY_CHUNK, :], q,
                            preferred_element_type=F32)
                pending.append((ui, ci) + probs(s, stab))
                if len(pending) > PIPELINE_DEPTH:
                    retire()
        while pending:
            retire()

    @pl.when(kv == pl.num_programs(2) - 1)
    def _finish():
        def normalised(acc, d):
            l = jnp.sum(acc[d:], axis=0, keepdims=True)
            return acc[:d] / l, l

        lmin = None
        for hd in range(A_HEADS):
            o, l = normalised(acc_a[hd], A_HEAD_DIM)
            lmin = l if lmin is None else jnp.minimum(lmin, l)
            oa_ref[0, hd * A_HEAD_DIM:(hd + 1) * A_HEAD_DIM, :] = o.astype(BF16)
        lam_c = lam_ref[...]
        lam = (jnp.exp(jnp.sum(lam_c[0:1] * lam_c[1:2], axis=1, keepdims=True))
               - jnp.exp(jnp.sum(lam_c[2:3] * lam_c[3:4], axis=1, keepdims=True)) + lam_init)
        for h in range(C_HEADS):
            (o1, l1), (o2, l2) = (normalised(acc_c[2 * h + j], C_V_DIM) for j in range(2))
            lmin = jnp.minimum(lmin, jnp.minimum(l1, l2))
            o = o1 - lam * o2
            ms = jnp.mean(o * o, axis=0, keepdims=True)
            o = o * lax.rsqrt(ms + EPS) * gsub_ref[...] * (1.0 - lam_init)
            oc_ref[0, h * C_V_DIM:(h + 1) * C_V_DIM, :] = o.astype(BF16)
        if not online:
            lmin_ref[0, 0] = jnp.broadcast_to(lmin, lmin_ref.shape[2:])


def _attention(stab, lam_c, g_subln, qa, qc, ka, kc, va, vc, lam_init, online):
    b, _, t = qa.shape
    tkv = ka.shape[2]
    if online:
        tq, tk = _pick(t, FALLBACK_QUERY_TILES), _pick(tkv, FALLBACK_KEY_TILES)
    else:
        tq, tk = _pick(t, QUERY_TILES), _pick(tkv, KEY_TILES)
    nq = t // tq
    in_specs = [pl.BlockSpec((4, C_QK_DIM), lambda bi, i, j: (0, 0)),
                pl.BlockSpec((C_V_DIM, 1), lambda bi, i, j: (0, 0)),
                pl.BlockSpec((1, A_WIDTH, tq), lambda bi, i, j: (bi, 0, i)),
                pl.BlockSpec((1, C_QK_WIDTH, tq), lambda bi, i, j: (bi, 0, i)),
                pl.BlockSpec((1, A_KV_HEADS, tk, A_HEAD_DIM), lambda bi, i, j: (bi, 0, j, 0)),
                pl.BlockSpec((1, 2 * C_HEADS, tk, C_QK_DIM), lambda bi, i, j: (bi, 0, j, 0)),
                pl.BlockSpec((1, A_KV_HEADS, A_HEAD_DIM, tk), lambda bi, i, j: (bi, 0, 0, j)),
                pl.BlockSpec((1, C_HEADS, C_V_DIM, tk), lambda bi, i, j: (bi, 0, 0, j))]
    out_specs = [pl.BlockSpec((1, A_WIDTH, tq), lambda bi, i, j: (bi, 0, i)),
                 pl.BlockSpec((1, C_WIDTH, tq), lambda bi, i, j: (bi, 0, i))]
    out_shape = [jax.ShapeDtypeStruct((b, A_WIDTH, t), BF16),
                 jax.ShapeDtypeStruct((b, C_WIDTH, t), BF16)]
    scratch = [pltpu.VMEM((A_HEADS, ACC_ROWS, tq), F32),
               pltpu.VMEM((2 * C_HEADS, ACC_ROWS, tq), F32)]
    args = (lam_c, g_subln, qa, qc, ka, kc, va, vc)
    if online:
        scratch += [pltpu.VMEM((A_HEADS, tq), F32), pltpu.VMEM((2 * C_HEADS, tq), F32)]
    else:
        in_specs = [pl.BlockSpec(memory_space=pltpu.SMEM)] + in_specs
        out_specs.append(pl.BlockSpec((1, 1, SUBLANES, tq), lambda bi, i, j: (bi, i, 0, 0)))
        out_shape.append(jax.ShapeDtypeStruct((b, nq, SUBLANES, tq), F32))
        args = (stab,) + args
    return pl.pallas_call(
        functools.partial(_attn_kernel, lam_init=lam_init, online=online),
        grid=(b, nq, tkv // tk),
        in_specs=in_specs, out_specs=out_specs, out_shape=out_shape, scratch_shapes=scratch,
        compiler_params=pltpu.CompilerParams(
            dimension_semantics=("parallel", "parallel", "arbitrary"),
            vmem_limit_bytes=VMEM_LIMIT),
        name="attention_online" if online else "attention_fixed",
    )(*args)


def _attend(gains_a, gains_c, lam_c, g_subln, qa, qc, ka, kc, va, vc, lam_init):
    bound = lambda gains, d: (1.02 * LOG2E * d ** 0.5
                              * jnp.max(jnp.abs(gains[0])) * jnp.max(jnp.abs(gains[1])))
    stab = jnp.stack([bound(gains_a, A_HEAD_DIM), bound(gains_c, C_QK_DIM)]) - STAB_MARGIN
    args = (lam_c, g_subln, qa, qc, ka, kc, va, vc, lam_init)
    oa, oc, lmin = _attention(stab.astype(F32), *args, online=False)
    return lax.cond(jnp.all(lmin >= MIN_DENOM), lambda: (oa, oc),
                    lambda: tuple(_attention(None, *args, online=True)))


def _merge_kernel(x_ref, gate_ref, oa_ref, oc_ref, rest_hbm, gsgu_ref, wsp_ref, bsp_ref,
                  wpa_ref, wpb_ref, wpc_ref, wout_ref, o_ref, mt_ref, rest_buf, rest_sem):
    tm = x_ref.shape[1]
    n_tiles = pl.num_programs(1)
    step = pl.program_id(0) * n_tiles + pl.program_id(1)
    n_steps = pl.num_programs(0) * n_tiles

    def rest_copy(s):
        slot = s % REST_SLOTS
        tok = pl.multiple_of((s % n_tiles) * tm, tm)
        return pltpu.make_async_copy(rest_hbm.at[s // n_tiles, :, pl.ds(tok, tm)],
                                     rest_buf.at[slot], rest_sem.at[slot])

    @pl.when(step == 0)
    def _prime():
        for s in range(REST_SLOTS - 1):
            @pl.when(s < n_steps)
            def _():
                rest_copy(s).start()

    @pl.when(step + REST_SLOTS - 1 < n_steps)
    def _prefetch():
        rest_copy(step + REST_SLOTS - 1).start()

    rest_copy(step).wait()
    rest_ref = rest_buf.at[step % REST_SLOTS]

    def merged(cs):
        n = cs.stop - cs.start
        rest = lambda r0, rows: rest_ref[r0:r0 + rows, cs].astype(F32)
        mixed_groups = []
        for g in range(B_GROUPS):
            v = rest(S_BV + g * B_GROUP_DIM, B_GROUP_DIM)
            ms = jnp.mean(v * v, axis=0, keepdims=True)
            vn = (v * lax.rsqrt(ms + EPS) * gsgu_ref[g]).astype(BF16)
            cols = [jnp.dot(vn[:, c:c + SGU_BD], wsp_ref[g], preferred_element_type=F32)
                    + bsp_ref[g] for c in range(0, n, SGU_BD)]
            mixed_groups.append(cols[0] if len(cols) == 1 else jnp.concatenate(cols, axis=1))
        ha = (oa_ref[0, :, cs].astype(F32) * _silu_of_twice(rest(S_ZA, A_WIDTH))).astype(BF16)
        ya = jnp.dot(wpa_ref[...], ha, preferred_element_type=F32)
        hc = (oc_ref[0, :, cs].astype(F32) * _silu_of_twice(rest(S_ZC, C_WIDTH))).astype(BF16)
        yc = jnp.dot(wpc_ref[...], hc, preferred_element_type=F32)
        mixed = jnp.concatenate(mixed_groups, axis=0)
        hb = (rest(S_BU, B_WIDTH) * mixed * _silu_of_twice(rest(S_BZ, B_WIDTH))).astype(BF16)
        yb = jnp.dot(wpb_ref[...], hb, preferred_element_type=F32)
        return (_twice_sigmoid_of_twice(rest(S_GATES, D_MODEL)) * ya
                + _twice_sigmoid_of_twice(rest(S_GATES + D_MODEL, D_MODEL)) * yb
                + _twice_sigmoid_of_twice(rest(S_GATES + 2 * D_MODEL, D_MODEL)) * yc)

    sub = min(tm, SUB_TILE)
    for c0 in range(0, tm, sub):
        cs = slice(c0, c0 + sub)
        mt_ref[cs, :] = merged(cs).T.astype(BF16)
        out = jnp.dot(mt_ref[cs, :], wout_ref[...], preferred_element_type=F32)
        o_ref[0, cs, :] = x_ref[0, cs, :] + gate_ref[0] * out


def _merge(x, gate, oa, oc, rest, gsgu, wsp_bd, bsp, wpa_t, wpb_t, wpc_t, wout):
    b, t, _ = x.shape
    tm = _pick(t, TOKEN_TILES[:2])
    tokT = lambda n: pl.BlockSpec((1, n, tm), lambda bi, i: (bi, 0, i))
    full = lambda shape: pl.BlockSpec(shape, lambda bi, i: (0,) * len(shape))
    return pl.pallas_call(
        _merge_kernel,
        grid=(b, t // tm),
        in_specs=[pl.BlockSpec((1, tm, D_MODEL), lambda bi, i: (bi, i, 0)),
                  pl.BlockSpec((1, 1, D_MODEL), lambda bi, i: (bi, 0, 0)),
                  tokT(A_WIDTH), tokT(C_WIDTH), pl.BlockSpec(memory_space=pl.ANY),
                  full((B_GROUPS, B_GROUP_DIM, 1)), full((B_GROUPS, SGU_BD, SGU_BD)),
                  full((B_GROUPS, 1, SGU_BD)),
                  full((D_MODEL, A_WIDTH)), full((D_MODEL, B_WIDTH)), full((D_MODEL, C_WIDTH)),
                  full((D_MODEL, D_MODEL))],
        out_specs=pl.BlockSpec((1, tm, D_MODEL), lambda bi, i: (bi, i, 0)),
        out_shape=jax.ShapeDtypeStruct((b, t, D_MODEL), F32),
        scratch_shapes=[pltpu.VMEM((tm, D_MODEL), BF16),
                        pltpu.VMEM((REST_SLOTS, REST_WIDTH, tm), BF16),
                        pltpu.SemaphoreType.DMA((REST_SLOTS,))],
        compiler_params=pltpu.CompilerParams(
            dimension_semantics=("arbitrary", "arbitrary"), vmem_limit_bytes=VMEM_LIMIT),
        name="merge_out",
    )(x, gate, oa, oc, rest, gsgu, wsp_bd, bsp, wpa_t, wpb_t, wpc_t, wout)


def _deinterleaved(w, n_heads):
    dim = w.shape[0] // n_heads
    w = w.reshape(n_heads, dim // 2, 2, *w.shape[1:])
    return jnp.swapaxes(w, 1, 2).reshape(n_heads * dim, *w.shape[3:])


def _projection_rows(w_in):
    off = np.cumsum([0, A_WIDTH, C_QK_WIDTH, A_KV_WIDTH, A_KV_WIDTH, C_QK_WIDTH, C_WIDTH])
    w = w_in.T
    qa, qc, ka, va, kc, vc = (w[off[i]:off[i + 1]] for i in range(6))
    half = np.ones((REST_WIDTH, 1), np.float32)
    half[S_ZA:S_BU] = 0.5
    half[S_BZ:] = 0.5
    return jnp.concatenate(
        [_deinterleaved(qa, A_HEADS), _deinterleaved(qc, 2 * C_HEADS),
         _deinterleaved(ka, A_KV_HEADS), _deinterleaved(kc, 2 * C_HEADS), va, vc,
         w[off[6]:] * half], axis=0).astype(BF16)


def _rope_tables(t, dim):
    quarter = dim // 4
    tok = jnp.arange(t)
    inv = jnp.power(ROPE_THETA, -jnp.arange(quarter, dtype=F32) / quarter)
    ang = jnp.concatenate([(tok // GRID_W)[:, None].astype(F32) * inv,
                           (tok % GRID_W)[:, None].astype(F32) * inv], axis=-1)
    return jnp.cos(ang).T, jnp.sin(ang).T


def kernel(x, c, ctx, c_ctx, w_mod, b_mod, g_norm, w_in, gq_a, gk_a, gq_c, gk_c, g_sgu, w_sp,
           b_sp, lam_c, g_subln, w_pa, w_pb, w_pc, w_out):
    depth = w_in.shape[0]
    b, t, _ = x.shape
    n_ctx = ctx.shape[1]

    cs = jnp.zeros((SUBLANES, D_MODEL), F32).at[:b].set(c).at[b].set(c_ctx)
    mod = _modulation(cs, w_mod, b_mod)

    cosa, sina = _rope_tables(t, A_HEAD_DIM)
    cosc, sinc = _rope_tables(t, C_QK_DIM)
    one = lambda n: jnp.ones((n, n_ctx), F32)
    zero = lambda n: jnp.zeros((n, n_ctx), F32)
    eye2 = jnp.eye(SGU_BD // B_CHUNK, dtype=F32)

    for l in range(depth):
        lam_init = 0.8 - 0.6 * math.exp(-0.3 * l)
        rowv = lambda v: v.reshape(-1, 1, D_MODEL)
        shift, scale, gate = (rowv(mod[l, :b, i * D_MODEL:(i + 1) * D_MODEL]) for i in range(3))
        shift_c, scale_c, gate_c = (
            rowv(jnp.broadcast_to(mod[l, b, i * D_MODEL:(i + 1) * D_MODEL], (b, D_MODEL)))
            for i in range(3))
        colv = lambda g: _deinterleaved(g, 1).reshape(-1, 1)
        proj = functools.partial(
            _project, g_norm=g_norm[l].reshape(1, D_MODEL), w_t=_projection_rows(w_in[l]),
            gqa=colv(gq_a[l] * (A_HEAD_DIM ** -0.5 * LOG2E)), gka=colv(gk_a[l]),
            gqc=colv(gq_c[l] * (C_QK_DIM ** -0.5 * LOG2E)), gkc=colv(gk_c[l]))

        qa, qc, ka, kc, va, vc, rest = proj(
            x, shift, scale, cosa=cosa, sina=sina, cosc=cosc, sinc=sinc, kv_len=t + n_ctx)
        qa_x, qc_x, ka, kc, va, vc, rest_x = proj(
            ctx, shift_c, scale_c, cosa=one(A_HEAD_DIM // 2), sina=zero(A_HEAD_DIM // 2),
            cosc=one(C_QK_DIM // 2), sinc=zero(C_QK_DIM // 2),
            kv_len=t + n_ctx, kv_start=t, kv_into=(ka, kc, va, vc))
        va, vc = va.reshape(b, A_KV_HEADS, -1, t + n_ctx), vc.reshape(b, C_HEADS, -1, t + n_ctx)

        attend = functools.partial(
            _attend, (gq_a[l], gk_a[l]), (gq_c[l], gk_c[l]), lam_c[l],
            g_subln[l].reshape(C_V_DIM, 1), lam_init=lam_init)
        oa, oc = attend(qa, qc, ka, kc, va, vc)

        wsp_bd = jnp.einsum("ij,gqp->giqjp", eye2, jnp.swapaxes(w_sp[l], 1, 2)).reshape(
            B_GROUPS, SGU_BD, SGU_BD).astype(BF16)
        bsp = jnp.tile(b_sp[l], (1, SGU_BD // B_CHUNK)).reshape(B_GROUPS, 1, SGU_BD)
        merge = functools.partial(
            _merge, gsgu=g_sgu[l].reshape(B_GROUPS, B_GROUP_DIM, 1), wsp_bd=wsp_bd, bsp=bsp,
            wpa_t=w_pa[l].T.astype(BF16), wpb_t=w_pb[l].T.astype(BF16),
            wpc_t=w_pc[l].T.astype(BF16), wout=(0.5 * w_out[l]).astype(BF16))
        x_new = merge(x, gate, oa, oc, rest)
        if l < depth - 1:
            oa_x, oc_x = attend(qa_x, qc_x, ka[:, :, t:], kc[:, :, t:], va[..., t:], vc[..., t:])
            ctx = merge(ctx, gate_c, oa_x, oc_x, rest_x)
        x = x_new
    return x
```
